```python
import jax, jax.numpy as jnp
from jax import lax
import numpy as np

D_MODEL = 2048
BATCH = 2
SEQ = 4096
DEPTH = 2
DEC_BATCH = 128
DEC_SEQ = 8
PAST_LEN = 16384
PAGE_SIZE = 128

N_MIXERS = 2
N_LRU_LAYERS = (DEPTH + 1) // 2
N_MLA_LAYERS = DEPTH // 2
N_DENSE_LAYERS = (DEPTH + 1) // 2
N_MOE_LAYERS = DEPTH // 2
LRU_WIDTH = D_MODEL
LRU_HEADS = 8
LRU_BLOCK = LRU_WIDTH // LRU_HEADS
CONV_WIDTH = 4
LRU_C = 8.0
MLA_HEADS = 16
QK_NOPE = 128
QK_ROPE = 64
V_HEAD = 128
Q_LORA = 512
KV_LORA = 512
ROPE_THETA = 10000.0
MLA_SCALE = (QK_NOPE + QK_ROPE) ** -0.5
Q_BLOCK = 128
D_FF = 3 * D_MODEL
N_EXPERTS = 8
TOP_K = 2
MOE_D_FF = 7 * D_MODEL // 2
PLE_DIM = 256
EPS = 1e-6
POOL_NUM = 5
POOL_DEN = 4

kernel_name = 'hybrid_rglru_mla_moe_step'

F32 = jnp.float32


def rms_norm(x, g):
    xf = x.astype(F32)
    y = xf * lax.rsqrt(jnp.mean(xf * xf, axis=-1, keepdims=True) + EPS)
    return (y * g.astype(F32)).astype(x.dtype)


def rope_cos_sin(pos):
    inv = 1.0 / (ROPE_THETA ** (jnp.arange(0, QK_ROPE, 2, dtype=F32) / QK_ROPE))
    ang = pos[:, None] * inv[None, :]
    return jnp.cos(ang), jnp.sin(ang)


def apply_rope(x, cos, sin):
    xf = x.astype(F32)
    x1, x2 = xf[..., :QK_ROPE // 2], xf[..., QK_ROPE // 2:]
    return jnp.concatenate([x1 * cos - x2 * sin, x2 * cos + x1 * sin], axis=-1).astype(x.dtype)


def rglru_block(xn, conv_state, h0, w_in, conv_w, conv_b, w_a, b_a, w_i, b_i, lam, w_out):
    B, T, _ = xn.shape
    proj = xn @ w_in
    xb, yb = proj[..., :LRU_WIDTH], proj[..., LRU_WIDTH:]
    yb = jax.nn.gelu(yb, approximate=True)
    xpad = jnp.concatenate([conv_state.astype(xb.dtype), xb], axis=1)
    xc = conv_b
    for k in range(CONV_WIDTH):
        xc = xc + xpad[:, k:k + T] * conv_w[k]
    new_conv = xpad[:, T:]
    xh = xc.reshape(B, T, LRU_HEADS, LRU_BLOCK)
    r = jax.nn.sigmoid((jnp.einsum('bthi,hij->bthj', xh, w_a) + b_a).astype(F32)).reshape(B, T, LRU_WIDTH)
    ig = jax.nn.sigmoid((jnp.einsum('bthi,hij->bthj', xh, w_i) + b_i).astype(F32)).reshape(B, T, LRU_WIDTH)
    log_a = -LRU_C * r * jax.nn.softplus(-lam.astype(F32))
    a = jnp.exp(log_a)
    mult = jnp.sqrt(-jnp.expm1(2.0 * log_a))
    bx = mult * ig * xc.astype(F32)
    bx = bx.at[:, 0].add(a[:, 0] * h0.astype(F32))

    def combine(left, right):
        a1, b1 = left
        a2, b2 = right
        return a1 * a2, a2 * b1 + b2

    _, h = lax.associative_scan(combine, (a, bx), axis=1)
    y = (h.astype(xn.dtype) * yb) @ w_out
    return y, new_conv, h[:, -1].astype(h0.dtype)


def mla_qkv(xn, cos, sin, wq_a, q_norm, wq_b, wkv_a, kv_norm):
    cq = rms_norm(xn @ wq_a, q_norm)
    q = jnp.einsum('btc,chd->bthd', cq, wq_b)
    q_nope = q[..., :QK_NOPE]
    q_pe = apply_rope(q[..., QK_NOPE:], cos[:, None, :], sin[:, None, :])
    kv = xn @ wkv_a
    ckv = rms_norm(kv[..., :KV_LORA], kv_norm)
    kpe = apply_rope(kv[..., KV_LORA:], cos, sin)
    return q_nope, q_pe, ckv, kpe


def mla_prompt(xn, wq_a, q_norm, wq_b, wkv_a, kv_norm, wkv_b, wo):
    B, S, _ = xn.shape
    cos, sin = rope_cos_sin(jnp.arange(S, dtype=F32))
    q_nope, q_pe, ckv, kpe = mla_qkv(xn, cos, sin, wq_a, q_norm, wq_b, wkv_a, kv_norm)
    k_nope = jnp.einsum('bsc,chd->bshd', ckv, wkv_b[..., :QK_NOPE])
    v = jnp.einsum('bsc,chd->bshd', ckv, wkv_b[..., QK_NOPE:])
    nb = S // Q_BLOCK
    qn_b = jnp.moveaxis(q_nope.reshape(B, nb, Q_BLOCK, MLA_HEADS, QK_NOPE), 1, 0)
    qp_b = jnp.moveaxis(q_pe.reshape(B, nb, Q_BLOCK, MLA_HEADS, QK_ROPE), 1, 0)
    key_pos = jnp.arange(S)

    def attend(args):
        qn, qp, start = args
        s = (jnp.einsum('bqhd,bshd->bhqs', qn, k_nope, preferred_element_type=F32)
             + jnp.einsum('bqhr,bsr->bhqs', qp, kpe, preferred_element_type=F32)) * MLA_SCALE
        q_pos = start + jnp.arange(Q_BLOCK)
        s = jnp.where(key_pos[None, :] <= q_pos[:, None], s, -jnp.inf)
        p = jax.nn.softmax(s, axis=-1)
        return jnp.einsum('bhqs,bshd->bqhd', p.astype(v.dtype), v)

    o = lax.map(attend, (qn_b, qp_b, jnp.arange(nb) * Q_BLOCK))
    o = jnp.moveaxis(o, 0, 1).reshape(B, S, MLA_HEADS * V_HEAD)
    return o @ wo, ckv, kpe


def online_update(m, l, acc, s, vals):
    m_new = jnp.maximum(m, jnp.max(s, axis=-1))
    corr = jnp.exp(m - m_new)
    p = jnp.exp(s - m_new[..., None])
    l = l * corr + jnp.sum(p, axis=-1)
    acc = acc * jnp.transpose(corr, (0, 2, 1))[..., None] + jnp.einsum('bhtp,bpc->bthc', p, vals.astype(F32))
    return m_new, l, acc


def mla_sample(xn, cache_ckv, cache_kpe, page_table, li, wq_a, q_norm, wq_b, wkv_a, kv_norm, wkv_b, wo):
    B, T, _ = xn.shape
    past_len = page_table.shape[1] * PAGE_SIZE
    cos, sin = rope_cos_sin(jnp.arange(T, dtype=F32) + past_len)
    q_nope, q_pe, ckv, kpe = mla_qkv(xn, cos, sin, wq_a, q_norm, wq_b, wkv_a, kv_norm)
    w_uk, w_uv = wkv_b[..., :QK_NOPE], wkv_b[..., QK_NOPE:]
    q_lat = jnp.einsum('bthd,chd->bthc', q_nope, w_uk)

    def page_step(carry, phys):
        m, l, acc = carry
        ck = cache_ckv[li, phys]
        kp = cache_kpe[li, phys]
        s = (jnp.einsum('bthc,bpc->bhtp', q_lat, ck, preferred_element_type=F32)
             + jnp.einsum('bthr,bpr->bhtp', q_pe, kp, preferred_element_type=F32)) * MLA_SCALE
        return online_update(m, l, acc, s, ck), None

    init = (jnp.full((B, MLA_HEADS, T), -jnp.inf, F32), jnp.zeros((B, MLA_HEADS, T), F32),
            jnp.zeros((B, T, MLA_HEADS, KV_LORA), F32))
    (m, l, acc), _ = lax.scan(page_step, init, page_table.T)
    s_new = (jnp.einsum('bthc,buc->bhtu', q_lat, ckv, preferred_element_type=F32)
             + jnp.einsum('bthr,bur->bhtu', q_pe, kpe, preferred_element_type=F32)) * MLA_SCALE
    causal = jnp.arange(T)[None, :] <= jnp.arange(T)[:, None]
    s_new = jnp.where(causal, s_new, -jnp.inf)
    m, l, acc = online_update(m, l, acc, s_new, ckv)
    o_lat = acc / jnp.transpose(l, (0, 2, 1))[..., None]
    o = jnp.einsum('bthc,chd->bthd', o_lat.astype(xn.dtype), w_uv).reshape(B, T, MLA_HEADS * V_HEAD)
    return o @ wo, ckv, kpe


def swiglu(x, w_gate, w_up, w_down):
    return (jax.nn.silu(x @ w_gate) * (x @ w_up)) @ w_down


def moe_ffn(xn, router, w_gate, w_up, w_down):
    logits = jnp.einsum('btd,de->bte', xn, router, preferred_element_type=F32)
    top_v, top_i = lax.top_k(logits, TOP_K)
    gates = jax.nn.softmax(top_v, axis=-1)
    combine = jnp.sum(jax.nn.one_hot(top_i, N_EXPERTS, dtype=F32) * gates[..., None], axis=-2)
    y = jnp.zeros_like(xn)
    for e in range(N_EXPERTS):
        y = y + combine[..., e:e + 1].astype(xn.dtype) * swiglu(xn, w_gate[e], w_up[e], w_down[e])
    return y


def ple_add(h, p, g, w_gate, b_gate, w_proj):
    gate = jax.nn.sigmoid((rms_norm(h, g) @ w_gate + b_gate).astype(F32)).astype(h.dtype)
    return h + gate * (p @ w_proj)


def setup_inputs(seed: int = 0) -> dict:
    key = jax.random.key(seed)
    ks = iter(jax.random.split(key, 48))

    def nrm(shape, scale):
        return jax.random.normal(next(ks), shape, F32) * scale

    def gain(shape):
        return 1.0 + nrm(shape, 0.02)

    n_pages = PAST_LEN // PAGE_SIZE
    n_phys = (POOL_NUM * DEC_BATCH * n_pages) // POOL_DEN
    x_prompt = nrm((BATCH, SEQ, D_MODEL), 1.0)
    x_sample = nrm((DEC_BATCH, DEC_SEQ, D_MODEL), 1.0)
    p_prompt = nrm((DEPTH, BATCH, SEQ, PLE_DIM), 1.0)
    p_sample = nrm((DEPTH, DEC_BATCH, DEC_SEQ, PLE_DIM), 1.0)
    cache_ckv = nrm((N_MLA_LAYERS, n_phys, PAGE_SIZE, KV_LORA), 1.0)
    cache_kpe = nrm((N_MLA_LAYERS, n_phys, PAGE_SIZE, QK_ROPE), 1.0)
    page_table = jax.random.permutation(next(ks), n_phys)[:DEC_BATCH * n_pages].reshape(DEC_BATCH, n_pages).astype(jnp.int32)
    state_h = nrm((N_LRU_LAYERS, DEC_BATCH, LRU_WIDTH), 0.5)
    state_conv = nrm((N_LRU_LAYERS, DEC_BATCH, CONV_WIDTH - 1, LRU_WIDTH), 1.0)
    norm_mix = gain((DEPTH, D_MODEL))
    norm_ffn = gain((DEPTH, D_MODEL))
    norm_ple = gain((DEPTH, D_MODEL))
    norm_final = gain((D_MODEL,))
    lru_w_in = nrm((N_LRU_LAYERS, D_MODEL, 2 * LRU_WIDTH), D_MODEL ** -0.5)
    lru_conv_w = nrm((N_LRU_LAYERS, CONV_WIDTH, LRU_WIDTH), CONV_WIDTH ** -0.5)
    lru_conv_b = nrm((N_LRU_LAYERS, LRU_WIDTH), 0.01)
    lru_w_a = nrm((N_LRU_LAYERS, LRU_HEADS, LRU_BLOCK, LRU_BLOCK), LRU_BLOCK ** -0.5)
    lru_b_a = nrm((N_LRU_LAYERS, LRU_HEADS, LRU_BLOCK), 0.01)
    lru_w_i = nrm((N_LRU_LAYERS, LRU_HEADS, LRU_BLOCK, LRU_BLOCK), LRU_BLOCK ** -0.5)
    lru_b_i = nrm((N_LRU_LAYERS, LRU_HEADS, LRU_BLOCK), 0.01)
    u = jax.random.uniform(next(ks), (N_LRU_LAYERS, LRU_WIDTH), F32, 0.9, 0.999)
    s_l = u ** (1.0 / LRU_C)
    lru_lambda = jnp.log(s_l) - jnp.log1p(-s_l)
    lru_w_out = nrm((N_LRU_LAYERS, LRU_WIDTH, D_MODEL), LRU_WIDTH ** -0.5)
    mla_wq_a = nrm((N_MLA_LAYERS, D_MODEL, Q_LORA), D_MODEL ** -0.5)
    mla_q_norm = gain((N_MLA_LAYERS, Q_LORA))
    mla_wq_b = nrm((N_MLA_LAYERS, Q_LORA, MLA_HEADS, QK_NOPE + QK_ROPE), Q_LORA ** -0.5)
    mla_wkv_a = nrm((N_MLA_LAYERS, D_MODEL, KV_LORA + QK_ROPE), D_MODEL ** -0.5)
    mla_kv_norm = gain((N_MLA_LAYERS, KV_LORA))
    mla_wkv_b = nrm((N_MLA_LAYERS, KV_LORA, MLA_HEADS, QK_NOPE + V_HEAD), KV_LORA ** -0.5)
    mla_wo = nrm((N_MLA_LAYERS, MLA_HEADS * V_HEAD, D_MODEL), (MLA_HEADS * V_HEAD) ** -0.5)
    ffn_w_gate = nrm((N_DENSE_LAYERS, D_MODEL, D_FF), D_MODEL ** -0.5)
    ffn_w_up = nrm((N_DENSE_LAYERS, D_MODEL, D_FF), D_MODEL ** -0.5)
    ffn_w_down = nrm((N_DENSE_LAYERS, D_FF, D_MODEL), D_FF ** -0.5)
    moe_router = nrm((N_MOE_LAYERS, D_MODEL, N_EXPERTS), D_MODEL ** -0.5)
    moe_w_gate = nrm((N_MOE_LAYERS, N_EXPERTS, D_MODEL, MOE_D_FF), D_MODEL ** -0.5)
    moe_w_up = nrm((N_MOE_LAYERS, N_EXPERTS, D_MODEL, MOE_D_FF), D_MODEL ** -0.5)
    moe_w_down = nrm((N_MOE_LAYERS, N_EXPERTS, MOE_D_FF, D_MODEL), MOE_D_FF ** -0.5)
    ple_w_gate = nrm((DEPTH, D_MODEL, D_MODEL), D_MODEL ** -0.5)
    ple_b_gate = nrm((DEPTH, D_MODEL), 0.01)
    ple_w_proj = nrm((DEPTH, PLE_DIM, D_MODEL), PLE_DIM ** -0.5)
    return {'x_prompt': x_prompt, 'x_sample': x_sample, 'p_prompt': p_prompt, 'p_sample': p_sample,
            'cache_ckv': cache_ckv, 'cache_kpe': cache_kpe, 'page_table': page_table,
            'state_h': state_h, 'state_conv': state_conv,
            'norm_mix': norm_mix, 'norm_ffn': norm_ffn, 'norm_ple': norm_ple, 'norm_final': norm_final,
            'lru_w_in': lru_w_in, 'lru_conv_w': lru_conv_w, 'lru_conv_b': lru_conv_b,
            'lru_w_a': lru_w_a, 'lru_b_a': lru_b_a, 'lru_w_i': lru_w_i, 'lru_b_i': lru_b_i,
            'lru_lambda': lru_lambda, 'lru_w_out': lru_w_out,
            'mla_wq_a': mla_wq_a, 'mla_q_norm': mla_q_norm, 'mla_wq_b': mla_wq_b, 'mla_wkv_a': mla_wkv_a,
            'mla_kv_norm': mla_kv_norm, 'mla_wkv_b': mla_wkv_b, 'mla_wo': mla_wo,
            'ffn_w_gate': ffn_w_gate, 'ffn_w_up': ffn_w_up, 'ffn_w_down': ffn_w_down,
            'moe_router': moe_router, 'moe_w_gate': moe_w_gate, 'moe_w_up': moe_w_up, 'moe_w_down': moe_w_down,
            'ple_w_gate': ple_w_gate, 'ple_b_gate': ple_b_gate, 'ple_w_proj': ple_w_proj}


def reference(x_prompt, x_sample, p_prompt, p_sample, cache_ckv, cache_kpe, page_table, state_h, state_conv,
              norm_mix, norm_ffn, norm_ple, norm_final,
              lru_w_in, lru_conv_w, lru_conv_b, lru_w_a, lru_b_a, lru_w_i, lru_b_i, lru_lambda, lru_w_out,
              mla_wq_a, mla_q_norm, mla_wq_b, mla_wkv_a, mla_kv_norm, mla_wkv_b, mla_wo,
              ffn_w_gate, ffn_w_up, ffn_w_down,
              moe_router, moe_w_gate, moe_w_up, moe_w_down,
              ple_w_gate, ple_b_gate, ple_w_proj):
    hp, hs = x_prompt, x_sample
    bp = hp.shape[0]
    ckv_p_l, kpe_p_l, ckv_s_l, kpe_s_l = [], [], [], []
    h_p_l, conv_p_l, h_s_l, conv_s_l = [], [], [], []
    for i in range(DEPTH):
        li = i // N_MIXERS
        xp = rms_norm(hp, norm_mix[i])
        xs = rms_norm(hs, norm_mix[i])
        if i % N_MIXERS == 0:
            lw = (lru_w_in[li], lru_conv_w[li], lru_conv_b[li], lru_w_a[li], lru_b_a[li],
                  lru_w_i[li], lru_b_i[li], lru_lambda[li], lru_w_out[li])
            zero_conv = jnp.zeros((bp, CONV_WIDTH - 1, LRU_WIDTH), hp.dtype)
            zero_h = jnp.zeros((bp, LRU_WIDTH), hp.dtype)
            yp, conv_p, h_p = rglru_block(xp, zero_conv, zero_h, *lw)
            ys, conv_s, h_s = rglru_block(xs, state_conv[li], state_h[li], *lw)
            h_p_l.append(h_p)
            conv_p_l.append(conv_p)
            h_s_l.append(h_s)
            conv_s_l.append(conv_s)
        else:
            mw = (mla_wq_a[li], mla_q_norm[li], mla_wq_b[li], mla_wkv_a[li], mla_kv_norm[li],
                  mla_wkv_b[li], mla_wo[li])
            yp, ckv_p, kpe_p = mla_prompt(xp, *mw)
            ys, ckv_s, kpe_s = mla_sample(xs, cache_ckv, cache_kpe, page_table, li, *mw)
            ckv_p_l.append(ckv_p)
            kpe_p_l.append(kpe_p)
            ckv_s_l.append(ckv_s)
            kpe_s_l.append(kpe_s)
        hp = hp + yp
        hs = hs + ys
        xp = rms_norm(hp, norm_ffn[i])
        xs = rms_norm(hs, norm_ffn[i])
        if i % 2 == 0:
            fp = swiglu(xp, ffn_w_gate[li], ffn_w_up[li], ffn_w_down[li])
            fs = swiglu(xs, ffn_w_gate[li], ffn_w_up[li], ffn_w_down[li])
        else:
            fp = moe_ffn(xp, moe_router[li], moe_w_gate[li], moe_w_up[li], moe_w_down[li])
            fs = moe_ffn(xs, moe_router[li], moe_w_gate[li], moe_w_up[li], moe_w_down[li])
        hp = hp + fp
        hs = hs + fs
        hp = ple_add(hp, p_prompt[i], norm_ple[i], ple_w_gate[i], ple_b_gate[i], ple_w_proj[i])
        hs = ple_add(hs, p_sample[i], norm_ple[i], ple_w_gate[i], ple_b_gate[i], ple_w_proj[i])
    y_prompt = rms_norm(hp, norm_final)
    y_sample = rms_norm(hs, norm_final)
    new_ckv_prompt = jnp.stack(ckv_p_l)
    new_kpe_prompt = jnp.stack(kpe_p_l)
    new_ckv_sample = jnp.stack(ckv_s_l)
    new_kpe_sample = jnp.stack(kpe_s_l)
    new_h_prompt = jnp.stack(h_p_l)
    new_conv_prompt = jnp.stack(conv_p_l)
    new_h_sample = jnp.stack(h_s_l)
    new_conv_sample = jnp.stack(conv_s_l)
    return (y_prompt, y_sample, new_ckv_prompt, new_kpe_prompt, new_ckv_sample, new_kpe_sample,
            new_h_prompt, new_conv_prompt, new_h_sample, new_conv_sample)
```

```python
import functools

import jax
import jax.numpy as jnp
from jax import lax
from jax.experimental import pallas as pl
from jax.experimental.pallas import tpu as pltpu

F32 = jnp.float32
BF16 = jnp.bfloat16

N_HEADS = 16
QK_NOPE = 128
QK_ROPE = 64
V_HEAD = 128
KV_LORA = 512
LRU_HEADS = 8
LRU_C = 8.0
CONV_WIDTH = 4
N_EXPERTS = 8
TOP_K = 2
PAGE = 128
ROPE_THETA = 10000.0
EPS = 1e-6
MLA_SCALE = (QK_NOPE + QK_ROPE) ** -0.5

LANES = 128
SUBLANES = 8
VMEM_LIMIT_BYTES = 56 * 1024 * 1024

NEG_INF = float("-inf")


def _cparams(*sem):
    return pltpu.CompilerParams(dimension_semantics=sem, vmem_limit_bytes=VMEM_LIMIT_BYTES)


def _pick(n, prefs):
    for p in prefs:
        if n % p == 0:
            return p
    return n


def _resnorm_kernel(*refs, has_add, want_sum, want_f32, want_bf16):
    it = iter(refs)
    h_ref = next(it)
    f_ref = next(it) if has_add else None
    gain_ref = next(it)
    hs = h_ref[...]
    if has_add:
        hs = hs + f_ref[...]
    if want_sum:
        next(it)[...] = hs
    y = hs * lax.rsqrt(jnp.mean(hs * hs, axis=-1, keepdims=True) + EPS)
    y = y * gain_ref[...]
    if want_f32:
        next(it)[...] = y
    if want_bf16:
        next(it)[...] = y.astype(BF16)


def _resnorm(h, gain, *, add=None, want_sum=False, want_f32=False, want_bf16=True, tm=None):
    m, d = h.shape
    tm = tm or _pick(m, (512, 256, 128, 64, 32, 16, 8))
    row = pl.BlockSpec((tm, d), lambda i: (i, 0))
    ins, specs = [h], [row]
    if add is not None:
        ins.append(add)
        specs.append(row)
    ins.append(gain.reshape(1, d))
    specs.append(pl.BlockSpec((1, d), lambda i: (0, 0)))
    outs, ospecs = [], []
    if want_sum:
        outs.append(jax.ShapeDtypeStruct((m, d), F32))
        ospecs.append(row)
    if want_f32:
        outs.append(jax.ShapeDtypeStruct((m, d), F32))
        ospecs.append(row)
    if want_bf16:
        outs.append(jax.ShapeDtypeStruct((m, d), BF16))
        ospecs.append(row)
    kern = functools.partial(_resnorm_kernel, has_add=add is not None,
                             want_sum=want_sum, want_f32=want_f32, want_bf16=want_bf16)
    return pl.pallas_call(kern, grid=(m // tm,), in_specs=specs, out_specs=ospecs, out_shape=outs,
                          compiler_params=_cparams("parallel"), name="resnorm")(*ins)


def _mm_kernel(*refs, nk, act, rope, has_bias, has_mul, has_res):
    it = iter(refs)
    x_ref = next(it)
    w_ref = next(it)
    w2_ref = next(it) if rope else None
    c_ref = next(it) if rope else None
    s_ref = next(it) if rope else None
    b_ref = next(it) if has_bias else None
    m_ref = next(it) if has_mul else None
    r_ref = next(it) if has_res else None
    o_ref = next(it)
    acc_ref = next(it) if nk > 1 else None

    x = x_ref[...].astype(BF16)
    p = jnp.dot(x, w_ref[...].astype(BF16), preferred_element_type=F32)
    if rope:
        p2 = jnp.dot(x, w2_ref[...].astype(BF16), preferred_element_type=F32)
        p = p * c_ref[...] + p2 * s_ref[...]

    def finish(acc):
        if has_bias:
            acc = acc + b_ref[...]
        if act == "sigmoid":
            acc = jax.nn.sigmoid(acc)
        if has_mul:
            acc = m_ref[...] * acc
        if has_res:
            acc = r_ref[...] + acc
        o_ref[...] = acc.astype(o_ref.dtype)

    if nk == 1:
        finish(p)
    else:
        k = pl.program_id(3)

        @pl.when(k == 0)
        def _():
            acc_ref[...] = p

        @pl.when(k > 0)
        def _():
            acc_ref[...] += p

        @pl.when(k == nk - 1)
        def _():
            finish(acc_ref[...])


def _mm(x, w, *, w2=None, cos=None, sin=None, bias=None, act=None, mul=None, res=None,
        out_dtype=F32, tm=None, tn=None, tk=None):
    squeeze = x.ndim == 2
    if squeeze:
        x, w = x[None], w[None]
        w2 = None if w2 is None else w2[None]
        cos = None if cos is None else cos[None]
        sin = None if sin is None else sin[None]
        bias = None if bias is None else bias.reshape(1, 1, -1)
        mul = None if mul is None else mul[None]
        res = None if res is None else res[None]
    g, m, kd = x.shape
    n = w.shape[-1]
    tm = tm or _pick(m, (1024, 512, 256, 128, 64, 32, 16, 8))
    tn = tn or _pick(n, (512, 256, 128))
    tk = tk or (kd if kd <= 2048 else _pick(kd, (1024, 512)))
    nk = kd // tk
    rope = w2 is not None
    xs = pl.BlockSpec((None, tm, tk), lambda gi, i, j, k: (gi, i, k))
    ws = pl.BlockSpec((None, tk, tn), lambda gi, i, j, k: (gi, k, j))
    os_ = pl.BlockSpec((None, tm, tn), lambda gi, i, j, k: (gi, i, j))
    ins, specs = [x, w], [xs, ws]
    if rope:
        ncb = cos.shape[-1] // tn
        tab = pl.BlockSpec((None, tm, tn), lambda gi, i, j, k: (gi, i, j % ncb))
        ins += [w2, cos, sin]
        specs += [ws, tab, tab]
    if bias is not None:
        ins.append(bias)
        specs.append(pl.BlockSpec((None, 1, tn), lambda gi, i, j, k: (gi, 0, j)))
    if mul is not None:
        ins.append(mul)
        specs.append(os_)
    if res is not None:
        ins.append(res)
        specs.append(os_)
    kern = functools.partial(_mm_kernel, nk=nk, act=act, rope=rope, has_bias=bias is not None,
                             has_mul=mul is not None, has_res=res is not None)
    scratch = [pltpu.VMEM((tm, tn), F32)] if nk > 1 else []
    out = pl.pallas_call(kern, grid=(g, m // tm, n // tn, nk), in_specs=specs, out_specs=os_,
                         out_shape=jax.ShapeDtypeStruct((g, m, n), out_dtype), scratch_shapes=scratch,
                         compiler_params=_cparams("parallel", "parallel", "parallel", "arbitrary"),
                         name="mm")(*ins)
    return out[0] if squeeze else out


def _ffn_kernel(te_ref, ns_ref, x_ref, wg_ref, wu_ref, wd_ref, o_ref, wg_s, wu_s, wd_s, *, sub):
    t = pl.program_id(0)
    j = pl.program_id(1)
    ns = ns_ref[t]

    @pl.when(j == 0)
    def _():
        o_ref[...] = jnp.zeros_like(o_ref)

    @pl.when(ns > 0)
    def _():
        wg_s[...] = wg_ref[...].astype(BF16)
        wu_s[...] = wu_ref[...].astype(BF16)
        wd_s[...] = wd_ref[...].astype(BF16)

        def body(s, carry):
            r = pl.multiple_of(s * sub, sub)
            xs = x_ref[pl.ds(r, sub), :].astype(BF16)
            gate = jnp.dot(xs, wg_s[...], preferred_element_type=F32)
            up = jnp.dot(xs, wu_s[...], preferred_element_type=F32)
            mid = (jax.nn.silu(gate) * up).astype(BF16)
            o_ref[pl.ds(r, sub), :] += jnp.dot(mid, wd_s[...], preferred_element_type=F32)
            return carry

        lax.fori_loop(0, ns, body, 0)


def _ffn(x, w_gate, w_up, w_down, tile_expert, tile_nsub, *, tm, tf, sub):
    p_rows, d = x.shape
    f = w_gate.shape[-1]
    nt, nj = p_rows // tm, f // tf

    def jj(t, j, ns):
        return jnp.where(ns[t] > 0, j, nj - 1)

    grid_spec = pltpu.PrefetchScalarGridSpec(
        num_scalar_prefetch=2, grid=(nt, nj),
        in_specs=[
            pl.BlockSpec((tm, d), lambda t, j, te, ns: (t, 0)),
            pl.BlockSpec((None, d, tf), lambda t, j, te, ns: (te[t], 0, jj(t, j, ns))),
            pl.BlockSpec((None, d, tf), lambda t, j, te, ns: (te[t], 0, jj(t, j, ns))),
            pl.BlockSpec((None, tf, d), lambda t, j, te, ns: (te[t], jj(t, j, ns), 0)),
        ],
        out_specs=pl.BlockSpec((tm, d), lambda t, j, te, ns: (t, 0)),
        scratch_shapes=[pltpu.VMEM((d, tf), BF16), pltpu.VMEM((d, tf), BF16), pltpu.VMEM((tf, d), BF16)])
    return pl.pallas_call(functools.partial(_ffn_kernel, sub=sub), grid_spec=grid_spec,
                          out_shape=jax.ShapeDtypeStruct((p_rows, d), F32),
                          compiler_params=_cparams("parallel", "arbitrary"),
                          name="ffn")(tile_expert, tile_nsub, x, w_gate, w_up, w_down)


def _softplus(x):
    return jnp.maximum(x, 0.0) + jnp.log1p(jnp.exp(-jnp.abs(x)))


def _lru_kernel(*refs, seg, carry, pre, rows):
    it = iter(refs)
    xb_ref = next(it)
    yb_ref = next(it)
    cw_ref = next(it)
    cb_ref = next(it)
    wa_ref = next(it)
    ba_ref = next(it)
    wi_ref = next(it)
    bi_ref = next(it)
    lam_ref = next(it)
    h0_ref = next(it) if pre else None
    hy_ref = next(it)
    h_ref = next(it)
    xpad_s = next(it)
    hc_s = next(it)

    c = pl.program_id(2)
    if carry:
        @pl.when(c == 0)
        def _():
            xpad_s[0:SUBLANES, :] = jnp.zeros((SUBLANES, xpad_s.shape[1]), F32)
            hc_s[...] = jnp.zeros_like(hc_s)
    else:
        xpad_s[0:SUBLANES, :] = jnp.zeros((SUBLANES, xpad_s.shape[1]), F32)

    x = xb_ref[...]
    xpad_s[pl.ds(SUBLANES, rows), :] = x
    cw = cw_ref[...]
    xc = cb_ref[...]
    for k in range(CONV_WIDTH):
        shift = CONV_WIDTH - 1 - k
        xk = x if shift == 0 else xpad_s[pl.ds(SUBLANES - shift, rows), :]
        xc = xc + xk * cw[k:k + 1, :]
    if carry:
        xpad_s[0:SUBLANES, :] = x[rows - SUBLANES:, :]

    xcb = xc.astype(BF16)
    ga = jnp.dot(xcb, wa_ref[...].astype(BF16), preferred_element_type=F32) + ba_ref[...]
    gi = jnp.dot(xcb, wi_ref[...].astype(BF16), preferred_element_type=F32) + bi_ref[...]
    r = jax.nn.sigmoid(ga)
    ig = jax.nn.sigmoid(gi)
    log_a = (-LRU_C * r) * _softplus(-lam_ref[...])
    a = jnp.exp(log_a)
    mult = jnp.sqrt(-jnp.tanh(log_a) * (a * a + 1.0))
    bx = mult * ig * xc

    row = lax.broadcasted_iota(jnp.int32, a.shape, 0)
    rmod = row & (seg - 1)
    if pre:
        valid = rmod >= seg // 2
        a = jnp.where(valid, a, 0.0)
        bx = jnp.where(valid, bx, h0_ref[...])

    av, bv = a, bx
    dist = 1
    while dist < seg:
        keep = rmod >= dist
        a_sh = jnp.where(keep, pltpu.roll(av, dist, 0), 1.0)
        b_sh = jnp.where(keep, pltpu.roll(bv, dist, 0), 0.0)
        bv = av * b_sh + bv
        av = av * a_sh
        dist *= 2
    if carry:
        h = av * hc_s[SUBLANES - 1:SUBLANES, :] + bv
        hc_s[...] = h[rows - SUBLANES:, :]
    else:
        h = bv
    hy_ref[...] = (h * jax.nn.gelu(yb_ref[...], approximate=True)).astype(BF16)
    if carry:
        h_ref[...] = h[rows - SUBLANES:, :]
    else:
        h_ref[...] = h


def _lru(proj, conv_w, conv_b, w_a, b_a, w_i, b_i, lam, *, h0_rows=None, chunk):
    bsz, t_len, w2 = proj.shape
    w = w2 // 2
    hb = w // LRU_HEADS
    pre = h0_rows is not None
    carry = not pre
    rows = chunk if carry else t_len
    seg = chunk if carry else 2 * SUBLANES
    nc = t_len // rows
    blk = lambda off: pl.BlockSpec((None, rows, hb), lambda b, h, c: (b, c, h + off))
    vec = pl.BlockSpec((1, hb), lambda b, h, c: (0, h))
    mat = pl.BlockSpec((None, hb, hb), lambda b, h, c: (h, 0, 0))
    hvec = pl.BlockSpec((None, 1, hb), lambda b, h, c: (h, 0, 0))
    ins = [proj, proj, conv_w, conv_b.reshape(1, w), w_a, b_a.reshape(LRU_HEADS, 1, hb),
           w_i, b_i.reshape(LRU_HEADS, 1, hb), lam.reshape(1, w)]
    specs = [blk(0), blk(LRU_HEADS), pl.BlockSpec((CONV_WIDTH, hb), lambda b, h, c: (0, h)), vec,
             mat, hvec, mat, hvec, vec]
    if pre:
        ins.append(h0_rows)
        specs.append(pl.BlockSpec((None, rows, hb), lambda b, h, c: (b, c, h)))
    h_rows = SUBLANES if carry else t_len
    outs = [jax.ShapeDtypeStruct((bsz, t_len, w), BF16), jax.ShapeDtypeStruct((bsz, h_rows, w), F32)]
    ospecs = [pl.BlockSpec((None, rows, hb), lambda b, h, c: (b, c, h)),
              pl.BlockSpec((None, h_rows, hb), lambda b, h, c: (b, 0, h))]
    kern = functools.partial(_lru_kernel, seg=seg, carry=carry, pre=pre, rows=rows)
    return pl.pallas_call(kern, grid=(bsz, LRU_HEADS, nc), in_specs=specs, out_specs=ospecs, out_shape=outs,
                          scratch_shapes=[pltpu.VMEM((rows + SUBLANES, hb), F32), pltpu.VMEM((SUBLANES, hb), F32)],
                          compiler_params=_cparams("parallel", "parallel", "arbitrary"), name="lru")(*ins)


def _flash_kernel(qi_ref, ki_ref, q_ref, kn_ref, kp_ref, v_ref, o_ref, m_s, l_s, acc_s, *, tq, tk):
    p_id = pl.program_id(2)
    qi = qi_ref[p_id]
    ki = ki_ref[p_id]

    @pl.when(ki == 0)
    def _():
        m_s[...] = jnp.full_like(m_s, NEG_INF)
        l_s[...] = jnp.zeros_like(l_s)
        acc_s[...] = jnp.zeros_like(acc_s)

    k = jnp.concatenate([kn_ref[...], kp_ref[...]], axis=1)
    s = lax.dot_general(q_ref[...], k, (((1,), (1,)), ((), ())), preferred_element_type=F32) * MLA_SCALE
    q_pos = qi * tq + lax.broadcasted_iota(jnp.int32, s.shape, 0)
    k_pos = ki * tk + lax.broadcasted_iota(jnp.int32, s.shape, 1)
    s = jnp.where(k_pos <= q_pos, s, NEG_INF)
    m_old = m_s[...]
    m_new = jnp.maximum(m_old, jnp.max(s, axis=1, keepdims=True))
    alpha = jnp.exp(m_old - m_new)
    p = jnp.exp(s - m_new)
    l_s[...] = alpha * l_s[...] + jnp.sum(p, axis=1, keepdims=True)
    acc_s[...] = alpha * acc_s[...] + jnp.dot(p.astype(BF16), v_ref[...], preferred_element_type=F32)
    m_s[...] = m_new

    @pl.when((ki + 1) * tk >= (qi + 1) * tq)
    def _():
        o_ref[...] = (acc_s[...] / l_s[...]).astype(o_ref.dtype)


def _flash(q_cat, kv_up, kpe, *, tq, tk):
    bsz, s_len, _ = q_cat.shape
    nq = s_len // tq
    pairs = [(qb, kb) for qb in range(nq) for kb in range(((qb + 1) * tq + tk - 1) // tk)]
    qidx = jnp.asarray([p[0] for p in pairs], jnp.int32)
    kidx = jnp.asarray([p[1] for p in pairs], jnp.int32)
    hd = QK_NOPE
    grid_spec = pltpu.PrefetchScalarGridSpec(
        num_scalar_prefetch=2, grid=(bsz, N_HEADS, len(pairs)),
        in_specs=[
            pl.BlockSpec((None, tq, 2 * hd), lambda b, h, p, qi, ki: (b, qi[p], h)),
            pl.BlockSpec((None, tk, hd), lambda b, h, p, qi, ki: (b, ki[p], h)),
            pl.BlockSpec((None, tk, hd), lambda b, h, p, qi, ki: (b, ki[p], 0)),
            pl.BlockSpec((None, tk, hd), lambda b, h, p, qi, ki: (b, ki[p], N_HEADS + h)),
        ],
        out_specs=pl.BlockSpec((None, tq, hd), lambda b, h, p, qi, ki: (b, qi[p], h)),
        scratch_shapes=[pltpu.VMEM((tq, 1), F32), pltpu.VMEM((tq, 1), F32), pltpu.VMEM((tq, hd), F32)])
    return pl.pallas_call(functools.partial(_flash_kernel, tq=tq, tk=tk), grid_spec=grid_spec,
                          out_shape=jax.ShapeDtypeStruct((bsz, s_len, N_HEADS * hd), BF16),
                          compiler_params=_cparams("parallel", "parallel", "arbitrary"),
                          name="flash")(qidx, kidx, q_cat, kv_up, kpe, kv_up)


def _paged_kernel(pt_ref, ql_ref, qp_ref, cn_ref, kn_ref, ckv_hbm, kpe_hbm, o_ref,
                  ck_buf, kp_buf, sems, m_s, l_s, acc_s, *, pp, n_steps, t_new):
    b_id = pl.program_id(0)
    s_id = pl.program_id(1)
    step = b_id * n_steps + s_id
    n_total = pl.num_programs(0) * n_steps
    slot = step % 2
    nt = (((1,), (1,)), ((), ()))

    def page_copies(for_step, for_slot):
        copies = []
        for i in range(pp):
            pg = pt_ref[for_step * pp + i]
            copies.append(pltpu.make_async_copy(ckv_hbm.at[pg], ck_buf.at[for_slot, i], sems.at[0, for_slot]))
            copies.append(pltpu.make_async_copy(kpe_hbm.at[pg], kp_buf.at[for_slot, i], sems.at[1, for_slot]))
        return copies

    @pl.when(step == 0)
    def _():
        for cp in page_copies(0, 0):
            cp.start()

    @pl.when(step + 1 < n_total)
    def _():
        for cp in page_copies(step + 1, 1 - slot):
            cp.start()

    @pl.when(s_id == 0)
    def _():
        m_s[...] = jnp.full_like(m_s, NEG_INF)
        l_s[...] = jnp.zeros_like(l_s)
        acc_s[...] = jnp.zeros_like(acc_s)

    for cp in page_copies(step, slot):
        cp.wait()

    ql = ql_ref[...]
    qp = qp_ref[...]

    def update(s, vals):
        m_old = m_s[...]
        m_new = jnp.maximum(m_old, jnp.max(s, axis=1, keepdims=True))
        alpha = jnp.exp(m_old - m_new)
        p = jnp.exp(s - m_new)
        l_s[...] = alpha * l_s[...] + jnp.sum(p, axis=1, keepdims=True)
        acc_s[...] = alpha * acc_s[...] + jnp.dot(p.astype(BF16), vals, preferred_element_type=F32)
        m_s[...] = m_new

    ck = ck_buf[slot].reshape(pp * PAGE, ck_buf.shape[-1]).astype(BF16)
    kp = kp_buf[slot].reshape(pp * PAGE, kp_buf.shape[-1]).astype(BF16)
    s = (lax.dot_general(ql, ck, nt, preferred_element_type=F32)
         + lax.dot_general(qp, kp, nt, preferred_element_type=F32)) * MLA_SCALE
    update(s, ck)

    @pl.when(s_id == n_steps - 1)
    def _():
        cn = cn_ref[...]
        s_new = (lax.dot_general(ql, cn, nt, preferred_element_type=F32)
                 + lax.dot_general(qp, kn_ref[...], nt, preferred_element_type=F32)) * MLA_SCALE
        tok = lax.broadcasted_iota(jnp.int32, s_new.shape, 0) % t_new
        key = lax.broadcasted_iota(jnp.int32, s_new.shape, 1)
        s_new = jnp.where(key <= tok, s_new, NEG_INF)
        update(s_new, cn)
        o_ref[...] = (acc_s[...] / l_s[...]).astype(o_ref.dtype)


def _paged_attention(q_lat, q_pe, c_new, k_new, cache_ckv, cache_kpe, page_table, *, pp):
    bsz, rows, c_dim = q_lat.shape
    r_dim = q_pe.shape[-1]
    n_pages = page_table.shape[1]
    n_steps = n_pages // pp
    t_new = rows // N_HEADS
    kpad = c_new.shape[1]
    qmap = lambda b, s, pt: (b, 0, 0)
    any_spec = pl.BlockSpec(memory_space=pl.ANY)
    in_specs = [pl.BlockSpec((None, rows, c_dim), qmap), pl.BlockSpec((None, rows, r_dim), qmap),
                pl.BlockSpec((None, kpad, c_dim), qmap), pl.BlockSpec((None, kpad, r_dim), qmap),
                any_spec, any_spec]
    grid_spec = pltpu.PrefetchScalarGridSpec(
        num_scalar_prefetch=1, grid=(bsz, n_steps), in_specs=in_specs,
        out_specs=pl.BlockSpec((None, rows, c_dim), qmap),
        scratch_shapes=[pltpu.VMEM((2, pp, PAGE, c_dim), F32), pltpu.VMEM((2, pp, PAGE, r_dim), F32),
                        pltpu.SemaphoreType.DMA((2, 2)),
                        pltpu.VMEM((rows, 1), F32), pltpu.VMEM((rows, 1), F32), pltpu.VMEM((rows, c_dim), F32)])
    kern = functools.partial(_paged_kernel, pp=pp, n_steps=n_steps, t_new=t_new)
    return pl.pallas_call(kern, grid_spec=grid_spec,
                          out_shape=jax.ShapeDtypeStruct((bsz, rows, c_dim), BF16),
                          compiler_params=_cparams("arbitrary", "arbitrary"), name="paged")(
        page_table.reshape(-1), q_lat, q_pe, c_new, k_new, cache_ckv, cache_kpe)


def _top2_kernel(lg_ref, idx_ref, gate_ref):
    lg = lg_ref[...]
    col = lax.broadcasted_iota(jnp.int32, lg.shape, 1).astype(F32)
    lg = jnp.where(col < N_EXPERTS, lg, NEG_INF)
    m1 = jnp.max(lg, axis=1, keepdims=True)
    i1 = jnp.min(jnp.where(lg == m1, col, float(LANES)), axis=1, keepdims=True)
    lg2 = jnp.where(col == i1, NEG_INF, lg)
    m2 = jnp.max(lg2, axis=1, keepdims=True)
    i2 = jnp.min(jnp.where(lg2 == m2, col, float(LANES)), axis=1, keepdims=True)
    e2 = jnp.exp(m2 - m1)
    den = 1.0 + e2
    idx_ref[...] = jnp.where(col == 0.0, i1, i2).astype(jnp.int32)
    gate_ref[...] = jnp.where(col == 0.0, 1.0 / den, e2 / den)


def _top2(logits):
    m = logits.shape[0]
    tm = _pick(m, (1024, 512, 256, 128, 64, 32, 16, 8))
    spec = pl.BlockSpec((tm, LANES), lambda i: (i, 0))
    return pl.pallas_call(_top2_kernel, grid=(m // tm,), in_specs=[spec], out_specs=[spec, spec],
                          out_shape=[jax.ShapeDtypeStruct((m, LANES), jnp.int32),
                                     jax.ShapeDtypeStruct((m, LANES), F32)],
                          compiler_params=_cparams("parallel"), name="top2")(logits)


ROW_COPY_WINDOW = 32


def _gather_into(src_hbm, dst_vmem, sem, n, src_row, dst_row):
    def copy(s_row, d_row):
        return pltpu.make_async_copy(src_hbm.at[pl.ds(s_row, 1), :], dst_vmem.at[pl.ds(d_row, 1), :], sem)

    def body(r, carry):
        copy(src_row(r), dst_row(r)).start()

        @pl.when(r >= ROW_COPY_WINDOW)
        def _():
            copy(0, 0).wait()

        return carry

    lax.fori_loop(0, n, body, 0)

    def drain(r, carry):
        copy(0, 0).wait()
        return carry

    lax.fori_loop(0, min(n, ROW_COPY_WINDOW), drain, 0)


def _gather_rows_kernel(idx_ref, src_ref, o_ref, sem, *, tm):
    base = pl.program_id(0) * tm
    _gather_into(src_ref, o_ref, sem, tm, lambda r: idx_ref[base + r], lambda r: r)


def _gather_rows(src, idx, *, tm):
    n = idx.shape[0]
    d = src.shape[1]
    grid_spec = pltpu.PrefetchScalarGridSpec(
        num_scalar_prefetch=1, grid=(n // tm,), in_specs=[pl.BlockSpec(memory_space=pl.ANY)],
        out_specs=pl.BlockSpec((tm, d), lambda t, idx: (t, 0)), scratch_shapes=[pltpu.SemaphoreType.DMA(())])
    return pl.pallas_call(functools.partial(_gather_rows_kernel, tm=tm), grid_spec=grid_spec,
                          out_shape=jax.ShapeDtypeStruct((n, d), src.dtype),
                          compiler_params=_cparams("arbitrary"), name="gather_rows")(idx, src)


def _combine_kernel(pos_ref, h_ref, gt_ref, gain_ref, ys_ref, hs_ref, hn_ref, g_s, sem, *, tm):
    base = pl.program_id(0) * tm * TOP_K
    _gather_into(ys_ref, g_s, sem, tm * TOP_K, lambda a: pos_ref[base + a],
                 lambda a: lax.rem(a, TOP_K) * tm + lax.div(a, TOP_K))
    gt = gt_ref[...]
    y = gt[:, 0:1] * g_s[0:tm, :] + gt[:, 1:2] * g_s[tm:2 * tm, :]
    hs = h_ref[...] + y
    hs_ref[...] = hs
    yn = hs * lax.rsqrt(jnp.mean(hs * hs, axis=-1, keepdims=True) + EPS)
    hn_ref[...] = (yn * gain_ref[...]).astype(BF16)


def _combine(h, ys, pos, gates, gain, *, tm):
    m, d = h.shape
    row = lambda w: pl.BlockSpec((tm, w), lambda i, pos: (i, 0))
    grid_spec = pltpu.PrefetchScalarGridSpec(
        num_scalar_prefetch=1, grid=(m // tm,),
        in_specs=[row(d), row(LANES), pl.BlockSpec((1, d), lambda i, pos: (0, 0)), pl.BlockSpec(memory_space=pl.ANY)],
        out_specs=[row(d), row(d)],
        scratch_shapes=[pltpu.VMEM((TOP_K * tm, d), F32), pltpu.SemaphoreType.DMA(())])
    return pl.pallas_call(functools.partial(_combine_kernel, tm=tm), grid_spec=grid_spec,
                          out_shape=[jax.ShapeDtypeStruct((m, d), F32), jax.ShapeDtypeStruct((m, d), BF16)],
                          compiler_params=_cparams("arbitrary"), name="combine")(
        pos, h, gates, gain.reshape(1, d), ys)


MOE_TILE = 1024
MOE_SUB = 256
MOE_TF = 256
FFN_TILE = 1024
LRU_CHUNK = 512
FLASH_TILE = 1024
PAGES_PER_STEP = 16
COMBINE_TILE = 256


def _moe(xn_f32, router, w_gate, w_up, w_down):
    m, d = xn_f32.shape
    n_exp = router.shape[-1]
    router_p = jnp.pad(router, ((0, 0), (0, LANES - n_exp)))
    logits = _mm(xn_f32, router_p, tn=LANES)
    idx, gates = _top2(logits)
    flat_e = idx[:, :TOP_K].reshape(-1)
    n_assign = flat_e.shape[0]
    onehot = (flat_e[:, None] == jnp.arange(n_exp, dtype=jnp.int32)[None, :]).astype(jnp.int32)
    csum = jnp.cumsum(onehot, axis=0)
    rank = jnp.take_along_axis(csum, flat_e[:, None], axis=1)[:, 0] - 1
    counts = csum[-1]
    tiles_e = (counts + MOE_TILE - 1) // MOE_TILE
    tile_end = jnp.cumsum(tiles_e)
    tile_start = tile_end - tiles_e
    pos = (tile_start[flat_e] * MOE_TILE + rank).astype(jnp.int32)
    n_tiles = n_assign // MOE_TILE + n_exp
    tid = jnp.arange(n_tiles, dtype=jnp.int32)
    te = jnp.minimum(jnp.searchsorted(tile_end, tid, side="right"), n_exp - 1).astype(jnp.int32)
    active = tid < tile_end[-1]
    last_e = jnp.max(jnp.where(counts > 0, jnp.arange(n_exp, dtype=jnp.int32), 0))
    te = jnp.where(active, te, last_e).astype(jnp.int32)
    rows_left = counts[te] - (tid - tile_start[te]) * MOE_TILE
    nsub = jnp.where(active, (jnp.clip(rows_left, 0, MOE_TILE) + MOE_SUB - 1) // MOE_SUB, 0).astype(jnp.int32)

    p_rows = n_tiles * MOE_TILE
    tok = (jnp.arange(n_assign, dtype=jnp.int32) // TOP_K).astype(jnp.int32)
    slot_tok = jnp.zeros((p_rows,), jnp.int32).at[pos].set(tok)
    xs = _gather_rows(xn_f32, slot_tok, tm=MOE_TILE)
    ys = _ffn(xs, w_gate, w_up, w_down, te, nsub, tm=MOE_TILE, tf=MOE_TF, sub=MOE_SUB)
    return ys, pos, gates


def _rope_tables(pos):
    inv = 1.0 / (ROPE_THETA ** (jnp.arange(0, QK_ROPE, 2, dtype=F32) / QK_ROPE))
    ang = pos[:, None] * inv[None, :]
    return jnp.cos(ang), jnp.sin(ang)


def _rot_cols(w):
    half = w.shape[-1] // 2
    return jnp.concatenate([-w[..., half:], w[..., :half]], axis=-1)


def kernel(x_prompt, x_sample, p_prompt, p_sample, cache_ckv, cache_kpe, page_table, state_h, state_conv, norm_mix, norm_ffn, norm_ple, norm_final, lru_w_in, lru_conv_w, lru_conv_b, lru_w_a, lru_b_a, lru_w_i, lru_b_i, lru_lambda, lru_w_out, mla_wq_a, mla_q_norm, mla_wq_b, mla_wkv_a, mla_kv_norm, mla_wkv_b, mla_wo, ffn_w_gate, ffn_w_up, ffn_w_down, moe_router, moe_w_gate, moe_w_up, moe_w_down, ple_w_gate, ple_b_gate, ple_w_proj):
    bp, sp, d = x_prompt.shape
    bs, ts, _ = x_sample.shape
    n_p, n_s = bp * sp, bs * ts
    w_lru = lru_w_out.shape[1]
    past_len = page_table.shape[1] * PAGE

    h = jnp.concatenate([x_prompt.reshape(n_p, d), x_sample.reshape(n_s, d)], axis=0)
    ple_in = jnp.concatenate([p_prompt.reshape(p_prompt.shape[0], n_p, -1),
                              p_sample.reshape(p_sample.shape[0], n_s, -1)], axis=1)

    def ple(h, hn, i):
        gate = _mm(hn, ple_w_gate[i], bias=ple_b_gate[i], act="sigmoid")
        return _mm(ple_in[i], ple_w_proj[i], mul=gate, res=h)

    (xn,) = _resnorm(h, norm_mix[0])
    proj = _mm(xn, lru_w_in[0])
    proj_p = proj[:n_p].reshape(bp, sp, 2 * w_lru)
    proj_s = proj[n_p:].reshape(bs, ts, 2 * w_lru)
    lw = (lru_conv_w[0], lru_conv_b[0], lru_w_a[0], lru_b_a[0], lru_w_i[0], lru_b_i[0], lru_lambda[0])
    hy_p, hl_p = _lru(proj_p, *lw, chunk=min(LRU_CHUNK, sp))
    zpad = jnp.zeros((bs, 2 * SUBLANES - ts - (CONV_WIDTH - 1), w_lru), F32)
    xb_s = jnp.concatenate([zpad, state_conv[0], proj_s[..., :w_lru]], axis=1)
    yb_s = jnp.concatenate([jnp.zeros((bs, 2 * SUBLANES - ts, w_lru), F32), proj_s[..., w_lru:]], axis=1)
    h0_rows = jnp.zeros((bs, 2 * SUBLANES, w_lru), F32).at[:, 2 * SUBLANES - ts - 1].set(state_h[0])
    seg_rows = bs * 2 * SUBLANES
    hy_s, hf_s = _lru(jnp.concatenate([xb_s, yb_s], axis=-1).reshape(1, seg_rows, 2 * w_lru), *lw,
                      h0_rows=h0_rows.reshape(1, seg_rows, w_lru), chunk=None)
    hy_s = hy_s.reshape(bs, 2 * SUBLANES, w_lru)[:, 2 * SUBLANES - ts:]
    hf_s = hf_s.reshape(bs, 2 * SUBLANES, w_lru)
    hy = jnp.concatenate([hy_p.reshape(n_p, w_lru), hy_s.reshape(n_s, w_lru)], axis=0)
    h = _mm(hy, lru_w_out[0], res=h)
    new_h_prompt = hl_p[:, SUBLANES - 1][None]
    new_h_sample = hf_s[:, -1][None]
    new_conv_prompt = proj_p[:, sp - (CONV_WIDTH - 1):, :w_lru][None]
    new_conv_sample = xb_s[:, 2 * SUBLANES - (CONV_WIDTH - 1):][None]

    (xn,) = _resnorm(h, norm_ffn[0])
    tm_ffn = _pick(h.shape[0], (FFN_TILE, 512, 256))
    nt_ffn = h.shape[0] // tm_ffn
    f = _ffn(xn, ffn_w_gate, ffn_w_up, ffn_w_down, jnp.zeros((nt_ffn,), jnp.int32),
             jnp.full((nt_ffn,), tm_ffn // MOE_SUB, jnp.int32), tm=tm_ffn, tf=MOE_TF, sub=MOE_SUB)
    h, hn = _resnorm(h, norm_ple[0], add=f, want_sum=True)
    h = ple(h, hn, 0)

    (xn,) = _resnorm(h, norm_mix[1])
    cq = _mm(xn, mla_wq_a[0])
    (cqn,) = _resnorm(cq, mla_q_norm[0])
    pos_all = jnp.concatenate([jnp.tile(jnp.arange(sp, dtype=F32), bp),
                               jnp.tile(jnp.arange(ts, dtype=F32) + past_len, bs)])
    cos, sin = _rope_tables(pos_all)
    zeros_r = jnp.zeros((n_p + n_s, LANES - QK_ROPE), F32)
    cos_k = jnp.concatenate([cos, cos, zeros_r], axis=1)
    sin_k = jnp.concatenate([sin, sin, zeros_r], axis=1)
    ones_n = jnp.ones((n_p + n_s, QK_NOPE), F32)
    cos_q = jnp.tile(jnp.concatenate([ones_n, cos_k], axis=1), (1, 2))
    sin_q = jnp.tile(jnp.concatenate([0.0 * ones_n, sin_k], axis=1), (1, 2))
    wq_b = mla_wq_b[0]
    cq_dim = wq_b.shape[0]
    wq_pe = wq_b[..., QK_NOPE:]
    zq = jnp.zeros((cq_dim, N_HEADS, LANES - QK_ROPE), F32)
    wq_main = jnp.concatenate([wq_b[..., :QK_NOPE], wq_pe, zq], axis=-1).reshape(cq_dim, N_HEADS * 2 * LANES)
    wq_rot = jnp.concatenate([jnp.zeros((cq_dim, N_HEADS, QK_NOPE), F32), _rot_cols(wq_pe), zq],
                             axis=-1).reshape(cq_dim, N_HEADS * 2 * LANES)
    q_cat = _mm(cqn, wq_main, w2=wq_rot, cos=cos_q, sin=sin_q, out_dtype=BF16, tn=4 * LANES)

    wkv_a = mla_wkv_a[0]
    ckv_raw = _mm(xn, wkv_a[:, :KV_LORA])
    ckv_f32, ckv_bf = _resnorm(ckv_raw, mla_kv_norm[0], want_f32=True)
    zk = jnp.zeros((d, LANES - QK_ROPE), F32)
    wk_pe = wkv_a[:, KV_LORA:]
    kpe_pad = _mm(xn, jnp.concatenate([wk_pe, zk], axis=1),
                  w2=jnp.concatenate([_rot_cols(wk_pe), zk], axis=1), cos=cos_k, sin=sin_k, tn=LANES)
    kpe_bf = kpe_pad.astype(BF16)
    new_ckv_prompt = ckv_f32[:n_p].reshape(1, bp, sp, KV_LORA)
    new_ckv_sample = ckv_f32[n_p:].reshape(1, bs, ts, KV_LORA)
    new_kpe_prompt = kpe_pad[:n_p, :QK_ROPE].reshape(1, bp, sp, QK_ROPE)
    new_kpe_sample = kpe_pad[n_p:, :QK_ROPE].reshape(1, bs, ts, QK_ROPE)

    wkv_b = mla_wkv_b[0]
    wkv_up = jnp.concatenate([wkv_b[..., :QK_NOPE].reshape(KV_LORA, -1), wkv_b[..., QK_NOPE:].reshape(KV_LORA, -1)],
                             axis=1)
    kv_up = _mm(ckv_bf[:n_p], wkv_up, out_dtype=BF16)
    t_att = _pick(sp, (FLASH_TILE, 512, 256, 128))
    o_p = _flash(q_cat[:n_p].reshape(bp, sp, -1), kv_up.reshape(bp, sp, -1), kpe_bf[:n_p].reshape(bp, sp, LANES),
                 tq=t_att, tk=t_att)

    q_s = q_cat[n_p:].reshape(bs, ts, N_HEADS, 2 * LANES)
    qn_s = jnp.transpose(q_s[..., :QK_NOPE], (2, 0, 1, 3)).reshape(N_HEADS, n_s, QK_NOPE)
    w_uk_t = jnp.transpose(wkv_b[..., :QK_NOPE], (1, 2, 0))
    q_lat = _mm(qn_s, w_uk_t, out_dtype=BF16)
    q_lat = jnp.transpose(q_lat.reshape(N_HEADS, bs, ts, KV_LORA), (1, 0, 2, 3)).reshape(bs, N_HEADS * ts, KV_LORA)
    q_pe = jnp.transpose(q_s[..., QK_NOPE:QK_NOPE + QK_ROPE], (0, 2, 1, 3)).reshape(bs, N_HEADS * ts, QK_ROPE)
    kpad = 2 * SUBLANES
    c_new = jnp.pad(ckv_bf[n_p:].reshape(bs, ts, KV_LORA), ((0, 0), (0, kpad - ts), (0, 0)))
    k_new = jnp.pad(kpe_bf[n_p:, :QK_ROPE].reshape(bs, ts, QK_ROPE), ((0, 0), (0, kpad - ts), (0, 0)))
    o_lat = _paged_attention(q_lat, q_pe, c_new, k_new, cache_ckv[0], cache_kpe[0], page_table,
                             pp=min(PAGES_PER_STEP, page_table.shape[1]))
    o_lat = jnp.transpose(o_lat.reshape(bs, N_HEADS, ts, KV_LORA), (1, 0, 2, 3)).reshape(N_HEADS, n_s, KV_LORA)
    w_uv = jnp.transpose(wkv_b[..., QK_NOPE:], (1, 0, 2))
    o_s = _mm(o_lat, w_uv, out_dtype=BF16, tn=V_HEAD)
    o_s = jnp.transpose(o_s, (1, 0, 2)).reshape(n_s, N_HEADS * V_HEAD)
    o_all = jnp.concatenate([o_p.reshape(n_p, -1), o_s], axis=0)
    h = _mm(o_all, mla_wo[0], res=h)

    (xn_f32,) = _resnorm(h, norm_ffn[1], want_f32=True, want_bf16=False)
    ys, pos, gates = _moe(xn_f32, moe_router[0], moe_w_gate[0], moe_w_up[0], moe_w_down[0])
    h, hn = _combine(h, ys, pos, gates, norm_ple[1], tm=_pick(h.shape[0], (COMBINE_TILE, 128, 64, 32, 16, 8)))
    h = ple(h, hn, 1)

    (y,) = _resnorm(h, norm_final, want_f32=True, want_bf16=False)
    y_prompt = y[:n_p].reshape(bp, sp, d)
    y_sample = y[n_p:].reshape(bs, ts, d)
    return (y_prompt, y_sample, new_ckv_prompt, new_kpe_prompt, new_ckv_sample, new_kpe_sample,
            new_h_prompt, new_conv_prompt, new_h_sample, new_conv_sample)
```

```python
import functools
import math

import jax
import jax.numpy as jnp
from jax import lax
from jax.experimental import pallas as pl
from jax.experimental.pallas import tpu as pltpu

F32 = jnp.float32
BF16 = jnp.bfloat16

N_HEADS = 16
QK_NOPE = 128
QK_ROPE = 64
V_HEAD = 128
KV_LORA = 512
LRU_HEADS = 8
LRU_C = 8.0
CONV_WIDTH = 4
N_EXPERTS = 8
TOP_K = 2
PAGE = 128
ROPE_THETA = 10000.0
EPS = 1e-6
MLA_SCALE = (QK_NOPE + QK_ROPE) ** -0.5
SCALE_LOG2E = MLA_SCALE * math.log2(math.e)

LANES = 128
SUBLANES = 8
VMEM_LIMIT_BYTES = 56 * 1024 * 1024

NEG_INF = float("-inf")


def _cparams(*sem):
    return pltpu.CompilerParams(dimension_semantics=sem, vmem_limit_bytes=VMEM_LIMIT_BYTES)


def _pick(n, prefs):
    for p in prefs:
        if n % p == 0:
            return p
    return n


def _resnorm_kernel(*refs, has_add, want_sum, want_f32, want_bf16):
    it = iter(refs)
    h_ref = next(it)
    f_ref = next(it) if has_add else None
    gain_ref = next(it)
    hs = h_ref[...]
    if has_add:
        hs = hs + f_ref[...]
    if want_sum:
        next(it)[...] = hs
    y = hs * lax.rsqrt(jnp.mean(hs * hs, axis=-1, keepdims=True) + EPS)
    y = y * gain_ref[...]
    if want_f32:
        next(it)[...] = y
    if want_bf16:
        next(it)[...] = y.astype(BF16)


def _resnorm(h, gain, *, add=None, want_sum=False, want_f32=False, want_bf16=True, tm=None, rows=None):
    d = h.shape[1]
    r0, m = rows if rows is not None else (0, h.shape[0])
    tm = tm or _pick(math.gcd(m, r0) if r0 else m, (512, 256, 128, 64, 32, 16, 8))
    b0 = r0 // tm
    row = pl.BlockSpec((tm, d), lambda i: (i, 0))
    row_in = pl.BlockSpec((tm, d), lambda i: (i + b0, 0))
    ins, specs = [h], [row_in]
    if add is not None:
        ins.append(add)
        specs.append(row_in)
    ins.append(gain.reshape(1, d))
    specs.append(pl.BlockSpec((1, d), lambda i: (0, 0)))
    outs, ospecs = [], []
    if want_sum:
        outs.append(jax.ShapeDtypeStruct((m, d), F32))
        ospecs.append(row)
    if want_f32:
        outs.append(jax.ShapeDtypeStruct((m, d), F32))
        ospecs.append(row)
    if want_bf16:
        outs.append(jax.ShapeDtypeStruct((m, d), BF16))
        ospecs.append(row)
    kern = functools.partial(_resnorm_kernel, has_add=add is not None,
                             want_sum=want_sum, want_f32=want_f32, want_bf16=want_bf16)
    return pl.pallas_call(kern, grid=(m // tm,), in_specs=specs, out_specs=ospecs, out_shape=outs,
                          compiler_params=_cparams("parallel"), name="resnorm")(*ins)


def _mm_kernel(*refs, nk, act, rope, has_bias, has_mul, has_res):
    it = iter(refs)
    x_ref = next(it)
    w_ref = next(it)
    w2_ref = next(it) if rope else None
    c_ref = next(it) if rope else None
    s_ref = next(it) if rope else None
    b_ref = next(it) if has_bias else None
    m_ref = next(it) if has_mul else None
    r_ref = next(it) if has_res else None
    o_ref = next(it)
    acc_ref = next(it) if nk > 1 else None

    x = x_ref[...].astype(BF16)
    p = jnp.dot(x, w_ref[...].astype(BF16), preferred_element_type=F32)
    if rope:
        p2 = jnp.dot(x, w2_ref[...].astype(BF16), preferred_element_type=F32)
        p = p * c_ref[...] + p2 * s_ref[...]

    def finish(acc):
        if has_bias:
            acc = acc + b_ref[...]
        if act == "sigmoid":
            acc = jax.nn.sigmoid(acc)
        if has_mul:
            acc = m_ref[...] * acc
        if has_res:
            acc = r_ref[...] + acc
        o_ref[...] = acc.astype(o_ref.dtype)

    if nk == 1:
        finish(p)
    else:
        k = pl.program_id(3)

        @pl.when(k == 0)
        def _():
            acc_ref[...] = p

        @pl.when(k > 0)
        def _():
            acc_ref[...] += p

        @pl.when(k == nk - 1)
        def _():
            finish(acc_ref[...])


def _mm(x, w, *, w2=None, cos=None, sin=None, bias=None, act=None, mul=None, res=None,
        out_dtype=F32, tm=None, tn=None, tk=None, m=None):
    squeeze = x.ndim == 2
    if squeeze:
        x, w = x[None], w[None]
        w2 = None if w2 is None else w2[None]
        cos = None if cos is None else cos[None]
        sin = None if sin is None else sin[None]
        bias = None if bias is None else bias.reshape(1, 1, -1)
        mul = None if mul is None else mul[None]
        res = None if res is None else res[None]
    g, m_all, kd = x.shape
    m = m or m_all
    n = w.shape[-1]
    tm = tm or _pick(m, (1024, 512, 256, 128, 64, 32, 16, 8))
    tn = tn or _pick(n, (512, 256, 128))
    tk = tk or (kd if kd <= 2048 else _pick(kd, (1024, 512)))
    nk = kd // tk
    rope = w2 is not None
    xs = pl.BlockSpec((None, tm, tk), lambda gi, i, j, k: (gi, i, k))
    ws = pl.BlockSpec((None, tk, tn), lambda gi, i, j, k: (gi, k, j))
    os_ = pl.BlockSpec((None, tm, tn), lambda gi, i, j, k: (gi, i, j))
    ins, specs = [x, w], [xs, ws]
    if rope:
        ncb = cos.shape[-1] // tn
        tab = pl.BlockSpec((None, tm, tn), lambda gi, i, j, k: (gi, i, j % ncb))
        ins += [w2, cos, sin]
        specs += [ws, tab, tab]
    if bias is not None:
        ins.append(bias)
        specs.append(pl.BlockSpec((None, 1, tn), lambda gi, i, j, k: (gi, 0, j)))
    if mul is not None:
        ins.append(mul)
        specs.append(os_)
    if res is not None:
        ins.append(res)
        specs.append(os_)
    kern = functools.partial(_mm_kernel, nk=nk, act=act, rope=rope, has_bias=bias is not None,
                             has_mul=mul is not None, has_res=res is not None)
    scratch = [pltpu.VMEM((tm, tn), F32)] if nk > 1 else []
    out = pl.pallas_call(kern, grid=(g, m // tm, n // tn, nk), in_specs=specs, out_specs=os_,
                         out_shape=jax.ShapeDtypeStruct((g, m, n), out_dtype), scratch_shapes=scratch,
                         compiler_params=_cparams("parallel", "parallel", "parallel", "arbitrary"),
                         name="mm")(*ins)
    return out[0] if squeeze else out


def _ffn_kernel(te_ref, ns_ref, x_ref, wg_ref, wu_ref, wd_ref, o_ref, wg_s, wu_s, wd_s, *, sub):
    t = pl.program_id(0)
    j = pl.program_id(1)
    ns = ns_ref[t]

    @pl.when(j == 0)
    def _():
        o_ref[...] = jnp.zeros_like(o_ref)

    def block(r, rows, wg, wu, wd):
        xs = x_ref[pl.ds(r, rows), :].astype(BF16)
        gate = jnp.dot(xs, wg, preferred_element_type=F32)
        up = jnp.dot(xs, wu, preferred_element_type=F32)
        mid = (jax.nn.silu(gate) * up).astype(BF16)
        o_ref[pl.ds(r, rows), :] += jnp.dot(mid, wd, preferred_element_type=F32)

    @pl.when(ns > 0)
    def _():
        wg = wg_ref[...].astype(BF16)
        wu = wu_ref[...].astype(BF16)
        wd = wd_ref[...].astype(BF16)
        wg_s[...] = wg
        wu_s[...] = wu
        wd_s[...] = wd
        block(0, sub, wg, wu, wd)
        rest = ns - 1
        n_pairs = lax.shift_right_logical(rest, 1)

        def pair(p, carry):
            block(pl.multiple_of(sub + p * (2 * sub), sub), 2 * sub, wg_s[...], wu_s[...], wd_s[...])
            return carry

        lax.fori_loop(0, n_pairs, pair, 0)

        @pl.when((rest & 1) == 1)
        def _():
            block(pl.multiple_of(sub + n_pairs * (2 * sub), sub), sub, wg_s[...], wu_s[...], wd_s[...])


def _ffn(x, w_gate, w_up, w_down, tile_expert, tile_nsub, *, tm, tf, sub):
    p_rows, d = x.shape
    f = w_gate.shape[-1]
    nt, nj = p_rows // tm, f // tf

    def jj(t, j, ns):
        return jnp.where(ns[t] > 0, j, nj - 1)

    grid_spec = pltpu.PrefetchScalarGridSpec(
        num_scalar_prefetch=2, grid=(nt, nj),
        in_specs=[
            pl.BlockSpec((tm, d), lambda t, j, te, ns: (t, 0)),
            pl.BlockSpec((None, d, tf), lambda t, j, te, ns: (te[t], 0, jj(t, j, ns))),
            pl.BlockSpec((None, d, tf), lambda t, j, te, ns: (te[t], 0, jj(t, j, ns))),
            pl.BlockSpec((None, tf, d), lambda t, j, te, ns: (te[t], jj(t, j, ns), 0)),
        ],
        out_specs=pl.BlockSpec((tm, d), lambda t, j, te, ns: (t, 0)),
        scratch_shapes=[pltpu.VMEM((d, tf), BF16), pltpu.VMEM((d, tf), BF16), pltpu.VMEM((tf, d), BF16)])
    return pl.pallas_call(functools.partial(_ffn_kernel, sub=sub), grid_spec=grid_spec,
                          out_shape=jax.ShapeDtypeStruct((p_rows, d), F32),
                          compiler_params=_cparams("parallel", "arbitrary"),
                          name="ffn")(tile_expert, tile_nsub, x, w_gate, w_up, w_down)


def _softplus(x):
    return jnp.maximum(x, 0.0) + jnp.log1p(jnp.exp(-jnp.abs(x)))


def _lru_kernel(*refs, seg, carry, pre, rows):
    it = iter(refs)
    xb_ref = next(it)
    yb_ref = next(it)
    cw_ref = next(it)
    cb_ref = next(it)
    wa_ref = next(it)
    ba_ref = next(it)
    wi_ref = next(it)
    bi_ref = next(it)
    lam_ref = next(it)
    h0_ref = next(it) if pre else None
    hy_ref = next(it)
    h_ref = next(it)
    xpad_s = next(it)
    hc_s = next(it)

    c = pl.program_id(2)
    if carry:
        @pl.when(c == 0)
        def _():
            xpad_s[0:SUBLANES, :] = jnp.zeros((SUBLANES, xpad_s.shape[1]), F32)
            hc_s[...] = jnp.zeros_like(hc_s)
    else:
        xpad_s[0:SUBLANES, :] = jnp.zeros((SUBLANES, xpad_s.shape[1]), F32)

    x = xb_ref[...]
    xpad_s[pl.ds(SUBLANES, rows), :] = x
    cw = cw_ref[...]
    xc = cb_ref[...]
    for k in range(CONV_WIDTH):
        shift = CONV_WIDTH - 1 - k
        xk = x if shift == 0 else xpad_s[pl.ds(SUBLANES - shift, rows), :]
        xc = xc + xk * cw[k:k + 1, :]
    if carry:
        xpad_s[0:SUBLANES, :] = x[rows - SUBLANES:, :]

    xcb = xc.astype(BF16)
    ga = jnp.dot(xcb, wa_ref[...].astype(BF16), preferred_element_type=F32) + ba_ref[...]
    gi = jnp.dot(xcb, wi_ref[...].astype(BF16), preferred_element_type=F32) + bi_ref[...]
    r = jax.nn.sigmoid(ga)
    ig = jax.nn.sigmoid(gi)
    log_a = (-LRU_C * r) * _softplus(-lam_ref[...])
    a = jnp.exp(log_a)
    mult = jnp.sqrt(-jnp.tanh(log_a) * (a * a + 1.0))
    bx = mult * ig * xc

    row = lax.broadcasted_iota(jnp.int32, a.shape, 0)
    rmod = row & (seg - 1)
    if pre:
        valid = rmod >= seg // 2
        a = jnp.where(valid, a, 0.0)
        bx = jnp.where(valid, bx, h0_ref[...])

    av, bv = a, bx
    dist = 1
    while dist < seg:
        keep = rmod >= dist
        a_sh = jnp.where(keep, pltpu.roll(av, dist, 0), 1.0)
        b_sh = jnp.where(keep, pltpu.roll(bv, dist, 0), 0.0)
        bv = av * b_sh + bv
        av = av * a_sh
        dist *= 2
    if carry:
        h = av * hc_s[SUBLANES - 1:SUBLANES, :] + bv
        hc_s[...] = h[rows - SUBLANES:, :]
    else:
        h = bv
    hy_ref[...] = (h * jax.nn.gelu(yb_ref[...], approximate=True)).astype(BF16)
    if carry:
        h_ref[...] = h[rows - SUBLANES:, :]
    else:
        h_ref[...] = h


def _lru(proj, conv_w, conv_b, w_a, b_a, w_i, b_i, lam, *, bsz, t_len, h0_rows=None, chunk=None):
    w = proj.shape[1] // 2
    hb = w // LRU_HEADS
    pre = h0_rows is not None
    carry = not pre
    rows = chunk if carry else t_len
    seg = chunk if carry else 2 * SUBLANES
    nc = t_len // rows
    blk = lambda off: pl.BlockSpec((rows, hb), lambda b, h, c: (b * nc + c, h + off))
    vec = pl.BlockSpec((1, hb), lambda b, h, c: (0, h))
    mat = pl.BlockSpec((None, hb, hb), lambda b, h, c: (h, 0, 0))
    hvec = pl.BlockSpec((None, 1, hb), lambda b, h, c: (h, 0, 0))
    ins = [proj, proj, conv_w, conv_b.reshape(1, w), w_a, b_a.reshape(LRU_HEADS, 1, hb),
           w_i, b_i.reshape(LRU_HEADS, 1, hb), lam.reshape(1, w)]
    specs = [blk(0), blk(LRU_HEADS), pl.BlockSpec((CONV_WIDTH, hb), lambda b, h, c: (0, h)), vec,
             mat, hvec, mat, hvec, vec]
    if pre:
        ins.append(h0_rows)
        specs.append(blk(0))
    h_rows = SUBLANES if carry else t_len
    outs = [jax.ShapeDtypeStruct((bsz * t_len, w), BF16), jax.ShapeDtypeStruct((bsz, h_rows, w), F32)]
    ospecs = [blk(0), pl.BlockSpec((None, h_rows, hb), lambda b, h, c: (b, 0, h))]
    kern = functools.partial(_lru_kernel, seg=seg, carry=carry, pre=pre, rows=rows)
    return pl.pallas_call(kern, grid=(bsz, LRU_HEADS, nc), in_specs=specs, out_specs=ospecs, out_shape=outs,
                          scratch_shapes=[pltpu.VMEM((rows + SUBLANES, hb), F32), pltpu.VMEM((SUBLANES, hb), F32)],
                          compiler_params=_cparams("parallel", "parallel", "arbitrary"), name="lru")(*ins)


def _softmax_step(s, vals, m, l, acc):
    m_new = jnp.maximum(m, jnp.max(s, axis=1, keepdims=True))
    alpha = jnp.exp2((m - m_new) * SCALE_LOG2E)
    p = jnp.exp2((s - m_new) * SCALE_LOG2E)
    l = alpha * l + jnp.sum(p, axis=1, keepdims=True)
    acc = alpha * acc + jnp.dot(p.astype(BF16), vals, preferred_element_type=F32)
    return m_new, l, acc


def _flash_kernel(qi_ref, ki_ref, q_ref, kn_ref, kp_ref, v_ref, o_ref, m_s, l_s, acc_s, *, t, rc):
    p_id = pl.program_id(2)
    qi = qi_ref[p_id]
    ki = ki_ref[p_id]

    @pl.when(ki == 0)
    def _():
        m_s[...] = jnp.full_like(m_s, NEG_INF)
        l_s[...] = jnp.zeros_like(l_s)
        acc_s[...] = jnp.zeros_like(acc_s)

    def block(diagonal):
        k = jnp.concatenate([kn_ref[...], kp_ref[...]], axis=1)
        n_chunks = t // rc

        def keys_of(c):
            return (c + 1) * rc if diagonal else t

        def scores(c):
            s = lax.dot_general(q_ref[pl.ds(c * rc, rc), :], k[:keys_of(c)], (((1,), (1,)), ((), ())),
                                preferred_element_type=F32)
            if diagonal:
                q_pos = c * rc + lax.broadcasted_iota(jnp.int32, s.shape, 0)
                k_pos = lax.broadcasted_iota(jnp.int32, s.shape, 1)
                s = jnp.where(k_pos <= q_pos, s, NEG_INF)
            return s

        s_next = scores(0)
        for c in range(n_chunks):
            rows = pl.ds(c * rc, rc)
            s = s_next
            if c + 1 < n_chunks:
                s_next = scores(c + 1)
            m, l, acc = _softmax_step(s, v_ref[0:keys_of(c), :], m_s[rows, :], l_s[rows, :], acc_s[rows, :])
            m_s[rows, :] = m
            l_s[rows, :] = l
            acc_s[rows, :] = acc

    @pl.when(ki == qi)
    def _():
        block(True)
        o_ref[...] = (acc_s[...] / l_s[...]).astype(o_ref.dtype)

    @pl.when(ki != qi)
    def _():
        block(False)


def _flash(q_cat, kv_up, kpe, *, bsz, s_len, t):
    tq = tk = t
    nq = nkb = s_len // t
    pairs = [(qb, kb) for qb in range(nq) for kb in range(qb + 1)]
    qidx = jnp.asarray([p[0] for p in pairs], jnp.int32)
    kidx = jnp.asarray([p[1] for p in pairs], jnp.int32)
    hd = QK_NOPE
    grid_spec = pltpu.PrefetchScalarGridSpec(
        num_scalar_prefetch=2, grid=(bsz, N_HEADS, len(pairs)),
        in_specs=[
            pl.BlockSpec((tq, 2 * hd), lambda b, h, p, qi, ki: (b * nq + qi[p], h)),
            pl.BlockSpec((tk, hd), lambda b, h, p, qi, ki: (b * nkb + ki[p], h)),
            pl.BlockSpec((tk, hd), lambda b, h, p, qi, ki: (b * nkb + ki[p], 0)),
            pl.BlockSpec((tk, hd), lambda b, h, p, qi, ki: (b * nkb + ki[p], N_HEADS + h)),
        ],
        out_specs=pl.BlockSpec((tq, hd), lambda b, h, p, qi, ki: (b * nq + qi[p], h)),
        scratch_shapes=[pltpu.VMEM((tq, 1), F32), pltpu.VMEM((tq, 1), F32), pltpu.VMEM((tq, hd), F32)])
    return pl.pallas_call(functools.partial(_flash_kernel, t=t, rc=min(FLASH_ROW_CHUNK, t)), grid_spec=grid_spec,
                          out_shape=jax.ShapeDtypeStruct((bsz * s_len, N_HEADS * hd), BF16),
                          compiler_params=_cparams("parallel", "parallel", "arbitrary"),
                          name="flash")(qidx, kidx, q_cat, kv_up, kpe, kv_up)


def _paged_kernel(pt_ref, ql_ref, qp_ref, cn_ref, kn_ref, ckv_hbm, kpe_hbm, o_ref,
                  ck_buf, kp_buf, sems, m_s, l_s, acc_s, *, nb, pp, cp, n_pages, t_new):
    n_steps = n_pages // pp
    s_id = pl.program_id(1)
    step = pl.program_id(0) * n_steps + s_id
    last_step = pl.num_programs(0) * n_steps - 1
    slot = lax.rem(step, PAGED_SLOTS)
    nt = (((1,), (1,)), ((), ()))

    def page_copies(for_step, for_slot):
        first_b = lax.div(for_step, n_steps) * nb
        first_page = lax.rem(for_step, n_steps) * pp
        copies = []
        for bi in range(nb):
            for i in range(pp):
                pg = pt_ref[(first_b + bi) * n_pages + first_page + i]
                dst = bi * pp + i
                copies.append(pltpu.make_async_copy(ckv_hbm.at[pg], ck_buf.at[for_slot, dst], sems.at[0, for_slot]))
                copies.append(pltpu.make_async_copy(kpe_hbm.at[pg], kp_buf.at[for_slot, dst], sems.at[1, for_slot]))
        return copies

    @pl.when(step == 0)
    def _():
        for dma in page_copies(0, 0):
            dma.start()
        for dma in page_copies(jnp.minimum(1, last_step), 1):
            dma.start()

    @pl.when(s_id == 0)
    def _():
        m_s[...] = jnp.full_like(m_s, NEG_INF)
        l_s[...] = jnp.zeros_like(l_s)
        acc_s[...] = jnp.zeros_like(acc_s)

    for dma in page_copies(step, slot):
        dma.wait()

    def scores(c, bi):
        first = bi * pp + c * cp
        ck = ck_buf[slot, first:first + cp].reshape(cp * PAGE, ck_buf.shape[-1]).astype(BF16)
        kp_t = jnp.concatenate([kp_buf[slot, first + i] for i in range(cp)], axis=1).astype(BF16)
        s = (lax.dot_general(ql_ref[bi], ck, nt, preferred_element_type=F32)
             + jnp.dot(qp_ref[bi], kp_t, preferred_element_type=F32))
        return s, ck

    items = [(c, bi) for c in range(pp // cp) for bi in range(nb)]
    state = [(m_s[bi], l_s[bi], acc_s[bi]) for bi in range(nb)]
    ahead = {i: scores(*items[i]) for i in range(min(nb, len(items)))}
    for i, (c, bi) in enumerate(items):
        if i + nb < len(items):
            ahead[i + nb] = scores(*items[i + nb])
        s, ck = ahead.pop(i)
        state[bi] = _softmax_step(s, ck, *state[bi])
    for bi in range(nb):
        m_s[bi], l_s[bi], acc_s[bi] = state[bi]

    for dma in page_copies(jnp.minimum(step + 2, last_step), lax.rem(step + 2, PAGED_SLOTS)):
        dma.start()

    @pl.when(s_id == n_steps - 1)
    def _():
        for bi in range(nb):
            cn = cn_ref[bi]
            s_new = (lax.dot_general(ql_ref[bi], cn, nt, preferred_element_type=F32)
                     + lax.dot_general(qp_ref[bi], kn_ref[bi], nt, preferred_element_type=F32))
            tok = lax.broadcasted_iota(jnp.int32, s_new.shape, 0) % t_new
            key = lax.broadcasted_iota(jnp.int32, s_new.shape, 1)
            s_new = jnp.where(key <= tok, s_new, NEG_INF)
            _, l, acc = _softmax_step(s_new, cn, m_s[bi], l_s[bi], acc_s[bi])
            o_ref[bi] = (acc / l).astype(o_ref.dtype)

    @pl.when(step == last_step)
    def _():
        for ahead_by in (1, 2):
            for dma in page_copies(last_step, lax.rem(step + ahead_by, PAGED_SLOTS)):
                dma.wait()


def _paged_attention(q_lat, q_pe, c_new, k_new, cache_ckv, cache_kpe_t, page_table, *, nb, pp):
    bsz, rows, c_dim = q_lat.shape
    r_dim = q_pe.shape[-1]
    n_pages = page_table.shape[1]
    t_new = rows // N_HEADS
    kpad = c_new.shape[1]
    qmap = lambda b, s, pt: (b, 0, 0)
    any_spec = pl.BlockSpec(memory_space=pl.ANY)
    in_specs = [pl.BlockSpec((nb, rows, c_dim), qmap), pl.BlockSpec((nb, rows, r_dim), qmap),
                pl.BlockSpec((nb, kpad, c_dim), qmap), pl.BlockSpec((nb, kpad, r_dim), qmap),
                any_spec, any_spec]
    grid_spec = pltpu.PrefetchScalarGridSpec(
        num_scalar_prefetch=1, grid=(bsz // nb, n_pages // pp), in_specs=in_specs,
        out_specs=pl.BlockSpec((nb, rows, c_dim), qmap),
        scratch_shapes=[pltpu.VMEM((PAGED_SLOTS, nb * pp, PAGE, c_dim), F32),
                        pltpu.VMEM((PAGED_SLOTS, nb * pp, r_dim, PAGE), F32),
                        pltpu.SemaphoreType.DMA((2, PAGED_SLOTS)),
                        pltpu.VMEM((nb, rows, 1), F32), pltpu.VMEM((nb, rows, 1), F32),
                        pltpu.VMEM((nb, rows, c_dim), F32)])
    kern = functools.partial(_paged_kernel, nb=nb, pp=pp, cp=min(PAGED_KEY_CHUNK_PAGES, pp), n_pages=n_pages,
                             t_new=t_new)
    return pl.pallas_call(kern, grid_spec=grid_spec,
                          out_shape=jax.ShapeDtypeStruct((bsz, rows, c_dim), BF16),
                          compiler_params=_cparams("arbitrary", "arbitrary"), name="paged")(
        page_table.reshape(-1), q_lat, q_pe, c_new, k_new, cache_ckv, cache_kpe_t)


def _top2_kernel(lg_ref, idx_ref, gate_ref):
    lg = lg_ref[...]
    col = lax.broadcasted_iota(jnp.int32, lg.shape, 1).astype(F32)
    lg = jnp.where(col < N_EXPERTS, lg, NEG_INF)
    m1 = jnp.max(lg, axis=1, keepdims=True)
    i1 = jnp.min(jnp.where(lg == m1, col, float(LANES)), axis=1, keepdims=True)
    lg2 = jnp.where(col == i1, NEG_INF, lg)
    m2 = jnp.max(lg2, axis=1, keepdims=True)
    i2 = jnp.min(jnp.where(lg2 == m2, col, float(LANES)), axis=1, keepdims=True)
    e2 = jnp.exp(m2 - m1)
    den = 1.0 + e2
    idx_ref[...] = jnp.where(col == 0.0, i1, i2).astype(jnp.int32)
    gate_ref[...] = jnp.where(col == 0.0, 1.0 / den, e2 / den)


def _top2(logits):
    m = logits.shape[0]
    tm = _pick(m, (1024, 512, 256, 128, 64, 32, 16, 8))
    spec = pl.BlockSpec((tm, LANES), lambda i: (i, 0))
    return pl.pallas_call(_top2_kernel, grid=(m // tm,), in_specs=[spec], out_specs=[spec, spec],
                          out_shape=[jax.ShapeDtypeStruct((m, LANES), jnp.int32),
                                     jax.ShapeDtypeStruct((m, LANES), F32)],
                          compiler_params=_cparams("parallel"), name="top2")(logits)


ROW_COPY_UNROLL = 8


def _gather_into(src_hbm, dst_vmem, sem, n, src_row, dst_row):
    def copy(s_row, d_row):
        return pltpu.make_async_copy(src_hbm.at[pl.ds(s_row, 1), :], dst_vmem.at[pl.ds(d_row, 1), :], sem)

    assert n % ROW_COPY_UNROLL == 0

    def issue(g, carry):
        for u in range(ROW_COPY_UNROLL):
            r = g * ROW_COPY_UNROLL + u
            copy(src_row(r), dst_row(r)).start(priority=u % 2)
        return carry

    lax.fori_loop(0, n // ROW_COPY_UNROLL, issue, 0)

    def drain(g, carry):
        for _ in range(ROW_COPY_UNROLL):
            copy(0, 0).wait()
        return carry

    lax.fori_loop(0, n // ROW_COPY_UNROLL, drain, 0)


def _gather_rows_kernel(idx_ref, src_ref, o_ref, sem, *, tm):
    base = pl.program_id(0) * tm
    _gather_into(src_ref, o_ref, sem, tm, lambda r: idx_ref[base + r], lambda r: r)


def _gather_rows(src, idx, *, tm):
    n = idx.shape[0]
    d = src.shape[1]
    grid_spec = pltpu.PrefetchScalarGridSpec(
        num_scalar_prefetch=1, grid=(n // tm,), in_specs=[pl.BlockSpec(memory_space=pl.ANY)],
        out_specs=pl.BlockSpec((tm, d), lambda t, idx: (t, 0)), scratch_shapes=[pltpu.SemaphoreType.DMA(())])
    return pl.pallas_call(functools.partial(_gather_rows_kernel, tm=tm), grid_spec=grid_spec,
                          out_shape=jax.ShapeDtypeStruct((n, d), src.dtype),
                          compiler_params=_cparams("arbitrary"), name="gather_rows")(idx, src)


def _combine_kernel(pos_ref, h_ref, gt_ref, gain_ref, ys_ref, hs_ref, hn_ref, g_s, sem, *, tm):
    base = pl.program_id(0) * tm * TOP_K
    _gather_into(ys_ref, g_s, sem, tm * TOP_K, lambda a: pos_ref[base + a],
                 lambda a: lax.rem(a, TOP_K) * tm + lax.div(a, TOP_K))
    gt = gt_ref[...]
    y = gt[:, 0:1] * g_s[0:tm, :] + gt[:, 1:2] * g_s[tm:2 * tm, :]
    hs = h_ref[...] + y
    hs_ref[...] = hs
    yn = hs * lax.rsqrt(jnp.mean(hs * hs, axis=-1, keepdims=True) + EPS)
    hn_ref[...] = (yn * gain_ref[...]).astype(BF16)


def _combine(h, ys, pos, gates, gain, *, tm):
    m, d = h.shape
    row = lambda w: pl.BlockSpec((tm, w), lambda i, pos: (i, 0))
    grid_spec = pltpu.PrefetchScalarGridSpec(
        num_scalar_prefetch=1, grid=(m // tm,),
        in_specs=[row(d), row(LANES), pl.BlockSpec((1, d), lambda i, pos: (0, 0)), pl.BlockSpec(memory_space=pl.ANY)],
        out_specs=[row(d), row(d)],
        scratch_shapes=[pltpu.VMEM((TOP_K * tm, d), F32), pltpu.SemaphoreType.DMA(())])
    return pl.pallas_call(functools.partial(_combine_kernel, tm=tm), grid_spec=grid_spec,
                          out_shape=[jax.ShapeDtypeStruct((m, d), F32), jax.ShapeDtypeStruct((m, d), BF16)],
                          compiler_params=_cparams("arbitrary"), name="combine")(
        pos, h, gates, gain.reshape(1, d), ys)


MOE_TILE = 1024
MOE_SUB = 256
MOE_TF = 256
FFN_TILE = 1024
LRU_CHUNK = 512
FLASH_TILE = 1024
FLASH_ROW_CHUNK = 256
PAGED_KEY_CHUNK_PAGES = 4
PAGES_PER_STEP = 16
COMBINE_TILE = 256
PAGED_BATCH = 2
PAGED_SLOTS = 3


def _moe(xn_f32, router, w_gate, w_up, w_down):
    m, d = xn_f32.shape
    n_exp = router.shape[-1]
    router_p = jnp.pad(router, ((0, 0), (0, LANES - n_exp)))
    logits = _mm(xn_f32, router_p, tn=LANES)
    idx, gates = _top2(logits)
    flat_e = idx[:, :TOP_K].reshape(-1)
    n_assign = flat_e.shape[0]
    onehot = (flat_e[:, None] == jnp.arange(n_exp, dtype=jnp.int32)[None, :]).astype(jnp.int32)
    csum = jnp.cumsum(onehot, axis=0)
    rank = jnp.take_along_axis(csum, flat_e[:, None], axis=1)[:, 0] - 1
    counts = csum[-1]
    tiles_e = (counts + MOE_TILE - 1) // MOE_TILE
    tile_end = jnp.cumsum(tiles_e)
    tile_start = tile_end - tiles_e
    pos = (tile_start[flat_e] * MOE_TILE + rank).astype(jnp.int32)
    n_tiles = n_assign // MOE_TILE + n_exp
    tid = jnp.arange(n_tiles, dtype=jnp.int32)
    te = jnp.minimum(jnp.sum((tile_end[None, :] <= tid[:, None]).astype(jnp.int32), axis=1), n_exp - 1)
    active = tid < tile_end[-1]
    last_e = jnp.max(jnp.where(counts > 0, jnp.arange(n_exp, dtype=jnp.int32), 0))
    te = jnp.where(active, te, last_e).astype(jnp.int32)
    rows_left = counts[te] - (tid - tile_start[te]) * MOE_TILE
    nsub = jnp.where(active, (jnp.clip(rows_left, 0, MOE_TILE) + MOE_SUB - 1) // MOE_SUB, 0).astype(jnp.int32)

    p_rows = n_tiles * MOE_TILE
    tok = (jnp.arange(n_assign, dtype=jnp.int32) // TOP_K).astype(jnp.int32)
    slot_tok = jnp.zeros((p_rows,), jnp.int32).at[pos].set(tok)
    xs = _gather_rows(xn_f32, slot_tok, tm=MOE_TILE)
    ys = _ffn(xs, w_gate, w_up, w_down, te, nsub, tm=MOE_TILE, tf=MOE_TF, sub=MOE_SUB)
    return ys, pos, gates


def _rope_tables(pos):
    inv = 1.0 / (ROPE_THETA ** (jnp.arange(0, QK_ROPE, 2, dtype=F32) / QK_ROPE))
    ang = pos[:, None] * inv[None, :]
    return jnp.cos(ang), jnp.sin(ang)


def _rot_cols(w):
    half = w.shape[-1] // 2
    return jnp.concatenate([-w[..., half:], w[..., :half]], axis=-1)


def kernel(x_prompt, x_sample, p_prompt, p_sample, cache_ckv, cache_kpe, page_table, state_h, state_conv, norm_mix, norm_ffn, norm_ple, norm_final, lru_w_in, lru_conv_w, lru_conv_b, lru_w_a, lru_b_a, lru_w_i, lru_b_i, lru_lambda, lru_w_out, mla_wq_a, mla_q_norm, mla_wq_b, mla_wkv_a, mla_kv_norm, mla_wkv_b, mla_wo, ffn_w_gate, ffn_w_up, ffn_w_down, moe_router, moe_w_gate, moe_w_up, moe_w_down, ple_w_gate, ple_b_gate, ple_w_proj):
    bp, sp, d = x_prompt.shape
    bs, ts, _ = x_sample.shape
    n_p, n_s = bp * sp, bs * ts
    w_lru = lru_w_out.shape[1]
    past_len = page_table.shape[1] * PAGE

    h = jnp.concatenate([x_prompt.reshape(n_p, d), x_sample.reshape(n_s, d)], axis=0)
    ple_in = jnp.concatenate([p_prompt.reshape(p_prompt.shape[0], n_p, -1),
                              p_sample.reshape(p_sample.shape[0], n_s, -1)], axis=1)

    def ple(h, hn, i):
        gate = _mm(hn, ple_w_gate[i], bias=ple_b_gate[i], act="sigmoid")
        return _mm(ple_in[i], ple_w_proj[i], mul=gate, res=h)

    (xn,) = _resnorm(h, norm_mix[0])
    proj = _mm(xn, lru_w_in[0])
    proj_s = proj[n_p:].reshape(bs, ts, 2 * w_lru)
    lw = (lru_conv_w[0], lru_conv_b[0], lru_w_a[0], lru_b_a[0], lru_w_i[0], lru_b_i[0], lru_lambda[0])
    hy_p, hl_p = _lru(proj, *lw, bsz=bp, t_len=sp, chunk=min(LRU_CHUNK, sp))
    zpad = jnp.zeros((bs, 2 * SUBLANES - ts - (CONV_WIDTH - 1), w_lru), F32)
    xb_s = jnp.concatenate([zpad, state_conv[0], proj_s[..., :w_lru]], axis=1)
    yb_s = jnp.concatenate([jnp.zeros((bs, 2 * SUBLANES - ts, w_lru), F32), proj_s[..., w_lru:]], axis=1)
    h0_rows = jnp.zeros((bs, 2 * SUBLANES, w_lru), F32).at[:, 2 * SUBLANES - ts - 1].set(state_h[0])
    seg_rows = bs * 2 * SUBLANES
    hy_s, hf_s = _lru(jnp.concatenate([xb_s, yb_s], axis=-1).reshape(seg_rows, 2 * w_lru), *lw,
                      bsz=1, t_len=seg_rows, h0_rows=h0_rows.reshape(seg_rows, w_lru))
    hy_s = hy_s.reshape(bs, 2 * SUBLANES, w_lru)[:, 2 * SUBLANES - ts:]
    hf_s = hf_s.reshape(bs, 2 * SUBLANES, w_lru)
    hy = jnp.concatenate([hy_p, hy_s.reshape(n_s, w_lru)], axis=0)
    h = _mm(hy, lru_w_out[0], res=h)
    new_h_prompt = hl_p[:, SUBLANES - 1][None]
    new_h_sample = hf_s[:, -1][None]
    new_conv_prompt = proj[:n_p].reshape(bp, sp, 2 * w_lru)[:, sp - (CONV_WIDTH - 1):, :w_lru][None]
    new_conv_sample = xb_s[:, 2 * SUBLANES - (CONV_WIDTH - 1):][None]

    (xn,) = _resnorm(h, norm_ffn[0])
    tm_ffn = _pick(h.shape[0], (FFN_TILE, 512, 256))
    nt_ffn = h.shape[0] // tm_ffn
    f = _ffn(xn, ffn_w_gate, ffn_w_up, ffn_w_down, jnp.zeros((nt_ffn,), jnp.int32),
             jnp.full((nt_ffn,), tm_ffn // MOE_SUB, jnp.int32), tm=tm_ffn, tf=MOE_TF, sub=MOE_SUB)
    h, hn = _resnorm(h, norm_ple[0], add=f, want_sum=True)
    h = ple(h, hn, 0)

    (xn,) = _resnorm(h, norm_mix[1])
    cq = _mm(xn, mla_wq_a[0])
    (cqn,) = _resnorm(cq, mla_q_norm[0])
    pos_all = jnp.concatenate([jnp.tile(jnp.arange(sp, dtype=F32), bp),
                               jnp.tile(jnp.arange(ts, dtype=F32) + past_len, bs)])
    cos, sin = _rope_tables(pos_all)
    zeros_r = jnp.zeros((n_p + n_s, LANES - QK_ROPE), F32)
    cos_k = jnp.concatenate([cos, cos, zeros_r], axis=1)
    sin_k = jnp.concatenate([sin, sin, zeros_r], axis=1)
    ones_n = jnp.ones((n_p + n_s, QK_NOPE), F32)
    cos_q = jnp.tile(jnp.concatenate([ones_n, cos_k], axis=1), (1, 2))
    sin_q = jnp.tile(jnp.concatenate([0.0 * ones_n, sin_k], axis=1), (1, 2))
    wq_b = mla_wq_b[0]
    cq_dim = wq_b.shape[0]
    wq_pe = wq_b[..., QK_NOPE:]
    zq = jnp.zeros((cq_dim, N_HEADS, LANES - QK_ROPE), F32)
    wq_main = jnp.concatenate([wq_b[..., :QK_NOPE], wq_pe, zq], axis=-1).reshape(cq_dim, N_HEADS * 2 * LANES)
    wq_rot = jnp.concatenate([jnp.zeros((cq_dim, N_HEADS, QK_NOPE), F32), _rot_cols(wq_pe), zq],
                             axis=-1).reshape(cq_dim, N_HEADS * 2 * LANES)
    q_cat = _mm(cqn, wq_main, w2=wq_rot, cos=cos_q, sin=sin_q, out_dtype=BF16, tn=4 * LANES)

    wkv_a = mla_wkv_a[0]
    ckv_raw = _mm(xn, wkv_a[:, :KV_LORA])
    ckv_f32, ckv_bf = _resnorm(ckv_raw, mla_kv_norm[0], want_f32=True)
    zk = jnp.zeros((d, LANES - QK_ROPE), F32)
    wk_pe = wkv_a[:, KV_LORA:]
    kpe_pad = _mm(xn, jnp.concatenate([wk_pe, zk], axis=1),
                  w2=jnp.concatenate([_rot_cols(wk_pe), zk], axis=1), cos=cos_k, sin=sin_k, tn=LANES)
    kpe_bf = kpe_pad.astype(BF16)
    new_ckv_prompt = ckv_f32[:n_p].reshape(1, bp, sp, KV_LORA)
    new_ckv_sample = ckv_f32[n_p:].reshape(1, bs, ts, KV_LORA)
    new_kpe_prompt = kpe_pad[:n_p, :QK_ROPE].reshape(1, bp, sp, QK_ROPE)
    new_kpe_sample = kpe_pad[n_p:, :QK_ROPE].reshape(1, bs, ts, QK_ROPE)

    wkv_b = mla_wkv_b[0]
    wkv_up = jnp.concatenate([wkv_b[..., :QK_NOPE].reshape(KV_LORA, -1), wkv_b[..., QK_NOPE:].reshape(KV_LORA, -1)],
                             axis=1)
    kv_up = _mm(ckv_bf, wkv_up, out_dtype=BF16, m=n_p)
    t_att = _pick(sp, (FLASH_TILE, 512, 256, 128))
    o_p = _flash(q_cat, kv_up, kpe_bf, bsz=bp, s_len=sp, t=t_att)

    q_s = q_cat[n_p:].reshape(bs, ts, N_HEADS, 2 * LANES)
    qn_s = jnp.transpose(q_s[..., :QK_NOPE], (2, 0, 1, 3)).reshape(N_HEADS, n_s, QK_NOPE)
    w_uk_t = jnp.transpose(wkv_b[..., :QK_NOPE], (1, 2, 0))
    q_lat = _mm(qn_s, w_uk_t, out_dtype=BF16)
    q_lat = jnp.transpose(q_lat.reshape(N_HEADS, bs, ts, KV_LORA), (1, 0, 2, 3)).reshape(bs, N_HEADS * ts, KV_LORA)
    q_pe = jnp.transpose(q_s[..., QK_NOPE:QK_NOPE + QK_ROPE], (0, 2, 1, 3)).reshape(bs, N_HEADS * ts, QK_ROPE)
    kpad = 2 * SUBLANES
    c_new = jnp.pad(ckv_bf[n_p:].reshape(bs, ts, KV_LORA), ((0, 0), (0, kpad - ts), (0, 0)))
    k_new = jnp.pad(kpe_bf[n_p:, :QK_ROPE].reshape(bs, ts, QK_ROPE), ((0, 0), (0, kpad - ts), (0, 0)))
    o_lat = _paged_attention(q_lat, q_pe, c_new, k_new, cache_ckv[0], jnp.swapaxes(cache_kpe[0], 1, 2), page_table,
                             nb=PAGED_BATCH, pp=min(PAGES_PER_STEP, page_table.shape[1]))
    o_lat = jnp.transpose(o_lat.reshape(bs, N_HEADS, ts, KV_LORA), (1, 0, 2, 3)).reshape(N_HEADS, n_s, KV_LORA)
    w_uv = jnp.transpose(wkv_b[..., QK_NOPE:], (1, 0, 2))
    o_s = _mm(o_lat, w_uv, out_dtype=BF16, tn=V_HEAD)
    o_s = jnp.transpose(o_s, (1, 0, 2)).reshape(n_s, N_HEADS * V_HEAD)
    o_all = jnp.concatenate([o_p, o_s], axis=0)
    h = _mm(o_all, mla_wo[0], res=h)

    (xn_f32,) = _resnorm(h, norm_ffn[1], want_f32=True, want_bf16=False)
    ys, pos, gates = _moe(xn_f32, moe_router[0], moe_w_gate[0], moe_w_up[0], moe_w_down[0])
    h, hn = _combine(h, ys, pos, gates, norm_ple[1], tm=_pick(h.shape[0], (COMBINE_TILE, 128, 64, 32, 16, 8)))
    h = ple(h, hn, 1)

    (y_prompt,) = _resnorm(h, norm_final, want_f32=True, want_bf16=False, rows=(0, n_p))
    (y_sample,) = _resnorm(h, norm_final, want_f32=True, want_bf16=False, rows=(n_p, n_s))
    y_prompt = y_prompt.reshape(bp, sp, d)
    y_sample = y_sample.reshape(bs, ts, d)
    return (y_prompt, y_sample, new_ckv_prompt, new_kpe_prompt, new_ckv_sample, new_kpe_sample,
            new_h_prompt, new_conv_prompt, new_h_sample, new_conv_sample)
```

```python
import functools
import math

import jax
import jax.numpy as jnp
from jax import lax
from jax.experimental import pallas as pl
from jax.experimental.pallas import tpu as pltpu

F32 = jnp.float32
BF16 = jnp.bfloat16

N_HEADS = 16
QK_NOPE = 128
QK_ROPE = 64
V_HEAD = 128
KV_LORA = 512
LRU_HEADS = 8
LRU_C = 8.0
CONV_WIDTH = 4
N_EXPERTS = 8
TOP_K = 2
PAGE = 128
ROPE_THETA = 10000.0
EPS = 1e-6
MLA_SCALE = (QK_NOPE + QK_ROPE) ** -0.5
SCALE_LOG2E = MLA_SCALE * math.log2(math.e)

LANES = 128
SUBLANES = 8
VMEM_LIMIT_BYTES = 56 * 1024 * 1024

NEG_INF = float("-inf")


def _cparams(*sem):
    return pltpu.CompilerParams(dimension_semantics=sem, vmem_limit_bytes=VMEM_LIMIT_BYTES)


def _pick(n, prefs):
    for p in prefs:
        if n % p == 0:
            return p
    return n


def _resnorm_kernel(*refs, has_add, want_sum, want_f32, want_bf16):
    it = iter(refs)
    h_ref = next(it)
    f_ref = next(it) if has_add else None
    gain_ref = next(it)
    hs = h_ref[...]
    if has_add:
        hs = hs + f_ref[...]
    if want_sum:
        next(it)[...] = hs
    y = hs * lax.rsqrt(jnp.mean(hs * hs, axis=-1, keepdims=True) + EPS)
    y = y * gain_ref[...]
    if want_f32:
        next(it)[...] = y
    if want_bf16:
        next(it)[...] = y.astype(BF16)


def _resnorm(h, gain, *, add=None, want_sum=False, want_f32=False, want_bf16=True, tm=None, rows=None):
    d = h.shape[1]
    r0, m = rows if rows is not None else (0, h.shape[0])
    tm = tm or _pick(math.gcd(m, r0) if r0 else m, (512, 256, 128, 64, 32, 16, 8))
    b0 = r0 // tm
    row = pl.BlockSpec((tm, d), lambda i: (i, 0))
    row_in = pl.BlockSpec((tm, d), lambda i: (i + b0, 0))
    ins, specs = [h], [row_in]
    if add is not None:
        ins.append(add)
        specs.append(row_in)
    ins.append(gain.reshape(1, d))
    specs.append(pl.BlockSpec((1, d), lambda i: (0, 0)))
    outs, ospecs = [], []
    if want_sum:
        outs.append(jax.ShapeDtypeStruct((m, d), F32))
        ospecs.append(row)
    if want_f32:
        outs.append(jax.ShapeDtypeStruct((m, d), F32))
        ospecs.append(row)
    if want_bf16:
        outs.append(jax.ShapeDtypeStruct((m, d), BF16))
        ospecs.append(row)
    kern = functools.partial(_resnorm_kernel, has_add=add is not None,
                             want_sum=want_sum, want_f32=want_f32, want_bf16=want_bf16)
    return pl.pallas_call(kern, grid=(m // tm,), in_specs=specs, out_specs=ospecs, out_shape=outs,
                          compiler_params=_cparams("parallel"), name="resnorm")(*ins)


def _mm_kernel(*refs, nk, act, rope, has_bias, has_mul, mul_is_product, has_res):
    it = iter(refs)
    x_ref = next(it)
    w_ref = next(it)
    w2_ref = next(it) if rope else None
    c_ref = next(it) if rope else None
    s_ref = next(it) if rope else None
    b_ref = next(it) if has_bias else None
    m_ref = next(it) if has_mul else None
    m_w_ref = next(it) if mul_is_product else None
    r_ref = next(it) if has_res else None
    o_ref = next(it)
    acc_ref = next(it) if nk > 1 else None

    x = x_ref[...].astype(BF16)
    p = jnp.dot(x, w_ref[...].astype(BF16), preferred_element_type=F32)
    if rope:
        p2 = jnp.dot(x, w2_ref[...].astype(BF16), preferred_element_type=F32)
        p = p * c_ref[...] + p2 * s_ref[...]

    def finish(acc):
        if has_bias:
            acc = acc + b_ref[...]
        if act == "sigmoid":
            acc = jax.nn.sigmoid(acc)
        if mul_is_product:
            acc = acc * jnp.dot(m_ref[...].astype(BF16), m_w_ref[...].astype(BF16), preferred_element_type=F32)
        elif has_mul:
            acc = m_ref[...] * acc
        if has_res:
            acc = r_ref[...] + acc
        o_ref[...] = acc.astype(o_ref.dtype)

    if nk == 1:
        finish(p)
    else:
        k = pl.program_id(3)

        @pl.when(k == 0)
        def _():
            acc_ref[...] = p

        @pl.when(k > 0)
        def _():
            acc_ref[...] += p

        @pl.when(k == nk - 1)
        def _():
            finish(acc_ref[...])


def _mm(x, w, *, w2=None, cos=None, sin=None, bias=None, act=None, mul=None, res=None,
        out_dtype=F32, tm=None, tn=None, tk=None, m=None):
    squeeze = x.ndim == 2
    mul_is_product = isinstance(mul, tuple)
    if squeeze:
        x, w = x[None], w[None]
        w2 = None if w2 is None else w2[None]
        cos = None if cos is None else cos[None]
        sin = None if sin is None else sin[None]
        bias = None if bias is None else bias.reshape(1, 1, -1)
        if mul_is_product:
            mul = (mul[0][None], mul[1][None])
        elif mul is not None:
            mul = mul[None]
        res = None if res is None else res[None]
    g, m_all, kd = x.shape
    m = m or m_all
    n = w.shape[-1]
    tm = tm or _pick(m, (1024, 512, 256, 128, 64, 32, 16, 8))
    tn = tn or _pick(n, (512, 256, 128))
    tk = tk or (kd if kd <= 2048 else _pick(kd, (1024, 512)))
    nk = kd // tk
    rope = w2 is not None
    xs = pl.BlockSpec((None, tm, tk), lambda gi, i, j, k: (gi, i, k))
    ws = pl.BlockSpec((None, tk, tn), lambda gi, i, j, k: (gi, k, j))
    os_ = pl.BlockSpec((None, tm, tn), lambda gi, i, j, k: (gi, i, j))
    ins, specs = [x, w], [xs, ws]
    if rope:
        ncb = cos.shape[-1] // tn
        tab = pl.BlockSpec((None, tm, tn), lambda gi, i, j, k: (gi, i, j % ncb))
        ins += [w2, cos, sin]
        specs += [ws, tab, tab]
    if bias is not None:
        ins.append(bias)
        specs.append(pl.BlockSpec((None, 1, tn), lambda gi, i, j, k: (gi, 0, j)))
    if mul_is_product:
        assert nk == 1
        k2 = mul[0].shape[-1]
        ins += [mul[0], mul[1]]
        specs += [pl.BlockSpec((None, tm, k2), lambda gi, i, j, k: (gi, i, 0)),
                  pl.BlockSpec((None, k2, tn), lambda gi, i, j, k: (gi, 0, j))]
    elif mul is not None:
        ins.append(mul)
        specs.append(os_)
    if res is not None:
        ins.append(res)
        specs.append(os_)
    kern = functools.partial(_mm_kernel, nk=nk, act=act, rope=rope, has_bias=bias is not None,
                             has_mul=mul is not None, mul_is_product=mul_is_product, has_res=res is not None)
    scratch = [pltpu.VMEM((tm, tn), F32)] if nk > 1 else []
    out = pl.pallas_call(kern, grid=(g, m // tm, n // tn, nk), in_specs=specs, out_specs=os_,
                         out_shape=jax.ShapeDtypeStruct((g, m, n), out_dtype), scratch_shapes=scratch,
                         compiler_params=_cparams("parallel", "parallel", "parallel", "arbitrary"),
                         name="mm")(*ins)
    return out[0] if squeeze else out


def _ffn_kernel(te_ref, ns_ref, x_ref, wg_ref, wu_ref, wd_ref, o_ref, wg_s, wu_s, wd_s, *, sub):
    t = pl.program_id(0)
    j = pl.program_id(1)
    ns = ns_ref[t]

    @pl.when(j == 0)
    def _():
        o_ref[...] = jnp.zeros_like(o_ref)

    def block(r, rows, wg, wu, wd):
        xs = x_ref[pl.ds(r, rows), :].astype(BF16)
        gate = jnp.dot(xs, wg, preferred_element_type=F32)
        up = jnp.dot(xs, wu, preferred_element_type=F32)
        mid = (jax.nn.silu(gate) * up).astype(BF16)
        o_ref[pl.ds(r, rows), :] += jnp.dot(mid, wd, preferred_element_type=F32)

    @pl.when(ns > 0)
    def _():
        wg = wg_ref[...].astype(BF16)
        wu = wu_ref[...].astype(BF16)
        wd = wd_ref[...].astype(BF16)
        wg_s[...] = wg
        wu_s[...] = wu
        wd_s[...] = wd
        block(0, sub, wg, wu, wd)
        rest = ns - 1
        n_pairs = lax.shift_right_logical(rest, 1)

        def pair(p, carry):
            block(pl.multiple_of(sub + p * (2 * sub), sub), 2 * sub, wg_s[...], wu_s[...], wd_s[...])
            return carry

        lax.fori_loop(0, n_pairs, pair, 0)

        @pl.when((rest & 1) == 1)
        def _():
            block(pl.multiple_of(sub + n_pairs * (2 * sub), sub), sub, wg_s[...], wu_s[...], wd_s[...])


def _ffn(x, w_gate, w_up, w_down, tile_expert, tile_nsub, *, tm, tf, sub):
    p_rows, d = x.shape
    f = w_gate.shape[-1]
    nt, nj = p_rows // tm, f // tf

    def jj(t, j, ns):
        return jnp.where(ns[t] > 0, j, nj - 1)

    once = pl.Buffered(1)
    grid_spec = pltpu.PrefetchScalarGridSpec(
        num_scalar_prefetch=2, grid=(nt, nj),
        in_specs=[
            pl.BlockSpec((tm, d), lambda t, j, te, ns: (t, 0), pipeline_mode=once),
            pl.BlockSpec((None, d, tf), lambda t, j, te, ns: (te[t], 0, jj(t, j, ns))),
            pl.BlockSpec((None, d, tf), lambda t, j, te, ns: (te[t], 0, jj(t, j, ns))),
            pl.BlockSpec((None, tf, d), lambda t, j, te, ns: (te[t], jj(t, j, ns), 0)),
        ],
        out_specs=pl.BlockSpec((tm, d), lambda t, j, te, ns: (t, 0), pipeline_mode=once),
        scratch_shapes=[pltpu.VMEM((d, tf), BF16), pltpu.VMEM((d, tf), BF16), pltpu.VMEM((tf, d), BF16)])
    return pl.pallas_call(functools.partial(_ffn_kernel, sub=sub), grid_spec=grid_spec,
                          out_shape=jax.ShapeDtypeStruct((p_rows, d), F32),
                          compiler_params=_cparams("parallel", "arbitrary"),
                          name="ffn")(tile_expert, tile_nsub, x, w_gate, w_up, w_down)


def _softplus(x):
    return jnp.maximum(x, 0.0) + jnp.log1p(jnp.exp(-jnp.abs(x)))


def _lru_kernel(*refs, seg, carry, pre, rows):
    it = iter(refs)
    xb_ref = next(it)
    yb_ref = next(it)
    cw_ref = next(it)
    cb_ref = next(it)
    wa_ref = next(it)
    ba_ref = next(it)
    wi_ref = next(it)
    bi_ref = next(it)
    lam_ref = next(it)
    h0_ref = next(it) if pre else None
    hy_ref = next(it)
    h_ref = next(it)
    xpad_s = next(it)
    hc_s = next(it)

    c = pl.program_id(2)
    if carry:
        @pl.when(c == 0)
        def _():
            xpad_s[0:SUBLANES, :] = jnp.zeros((SUBLANES, xpad_s.shape[1]), F32)
            hc_s[...] = jnp.zeros_like(hc_s)
    else:
        xpad_s[0:SUBLANES, :] = jnp.zeros((SUBLANES, xpad_s.shape[1]), F32)

    x = xb_ref[...]
    xpad_s[pl.ds(SUBLANES, rows), :] = x
    cw = cw_ref[...]
    xc = cb_ref[...]
    for k in range(CONV_WIDTH):
        shift = CONV_WIDTH - 1 - k
        xk = x if shift == 0 else xpad_s[pl.ds(SUBLANES - shift, rows), :]
        xc = xc + xk * cw[k:k + 1, :]
    if carry:
        xpad_s[0:SUBLANES, :] = x[rows - SUBLANES:, :]

    xcb = xc.astype(BF16)
    ga = jnp.dot(xcb, wa_ref[...].astype(BF16), preferred_element_type=F32) + ba_ref[...]
    gi = jnp.dot(xcb, wi_ref[...].astype(BF16), preferred_element_type=F32) + bi_ref[...]
    r = jax.nn.sigmoid(ga)
    ig = jax.nn.sigmoid(gi)
    log_a = (-LRU_C * r) * _softplus(-lam_ref[...])
    a = jnp.exp(log_a)
    mult = jnp.sqrt(-jnp.tanh(log_a) * (a * a + 1.0))
    bx = mult * ig * xc

    row = lax.broadcasted_iota(jnp.int32, a.shape, 0)
    rmod = row & (seg - 1)
    if pre:
        valid = rmod >= seg // 2
        a = jnp.where(valid, a, 0.0)
        bx = jnp.where(valid, bx, h0_ref[...])

    av, bv = a, bx
    dist = 1
    while dist < seg:
        keep = rmod >= dist
        a_sh = jnp.where(keep, pltpu.roll(av, dist, 0), 1.0)
        b_sh = jnp.where(keep, pltpu.roll(bv, dist, 0), 0.0)
        bv = av * b_sh + bv
        av = av * a_sh
        dist *= 2
    if carry:
        h = av * hc_s[SUBLANES - 1:SUBLANES, :] + bv
        hc_s[...] = h[rows - SUBLANES:, :]
    else:
        h = bv
    hy_ref[...] = (h * jax.nn.gelu(yb_ref[...], approximate=True)).astype(BF16)
    if carry:
        h_ref[...] = h[rows - SUBLANES:, :]
    else:
        h_ref[...] = h


def _lru(proj, conv_w, conv_b, w_a, b_a, w_i, b_i, lam, *, bsz, t_len, h0_rows=None, chunk=None):
    w = proj.shape[1] // 2
    hb = w // LRU_HEADS
    pre = h0_rows is not None
    carry = not pre
    rows = chunk if carry else t_len
    seg = chunk if carry else 2 * SUBLANES
    nc = t_len // rows
    blk = lambda off: pl.BlockSpec((rows, hb), lambda b, h, c: (b * nc + c, h + off))
    vec = pl.BlockSpec((1, hb), lambda b, h, c: (0, h))
    mat = pl.BlockSpec((None, hb, hb), lambda b, h, c: (h, 0, 0))
    hvec = pl.BlockSpec((None, 1, hb), lambda b, h, c: (h, 0, 0))
    ins = [proj, proj, conv_w, conv_b.reshape(1, w), w_a, b_a.reshape(LRU_HEADS, 1, hb),
           w_i, b_i.reshape(LRU_HEADS, 1, hb), lam.reshape(1, w)]
    specs = [blk(0), blk(LRU_HEADS), pl.BlockSpec((CONV_WIDTH, hb), lambda b, h, c: (0, h)), vec,
             mat, hvec, mat, hvec, vec]
    if pre:
        ins.append(h0_rows)
        specs.append(blk(0))
    h_rows = SUBLANES if carry else t_len
    outs = [jax.ShapeDtypeStruct((bsz * t_len, w), BF16), jax.ShapeDtypeStruct((bsz, h_rows, w), F32)]
    ospecs = [blk(0), pl.BlockSpec((None, h_rows, hb), lambda b, h, c: (b, 0, h))]
    kern = functools.partial(_lru_kernel, seg=seg, carry=carry, pre=pre, rows=rows)
    return pl.pallas_call(kern, grid=(bsz, LRU_HEADS, nc), in_specs=specs, out_specs=ospecs, out_shape=outs,
                          scratch_shapes=[pltpu.VMEM((rows + SUBLANES, hb), F32), pltpu.VMEM((SUBLANES, hb), F32)],
                          compiler_params=_cparams("parallel", "parallel", "arbitrary"), name="lru")(*ins)


def _softmax_step(s, vals, m, l, acc):
    m_new = jnp.maximum(m, jnp.max(s, axis=1, keepdims=True))
    alpha = jnp.exp2((m - m_new) * SCALE_LOG2E)
    p = jnp.exp2((s - m_new) * SCALE_LOG2E)
    l = alpha * l + jnp.sum(p, axis=1, keepdims=True)
    acc = alpha * acc + jnp.dot(p.astype(BF16), vals, preferred_element_type=F32)
    return m_new, l, acc


def _flash_kernel(qi_ref, ki_ref, q_ref, kn_ref, kp_ref, v_ref, o_ref, m_s, l_s, acc_s, *, t, rc):
    p_id = pl.program_id(2)
    qi = qi_ref[p_id]
    ki = ki_ref[p_id]

    @pl.when(ki == 0)
    def _():
        m_s[...] = jnp.full_like(m_s, NEG_INF)
        l_s[...] = jnp.zeros_like(l_s)
        acc_s[...] = jnp.zeros_like(acc_s)

    def block(diagonal):
        k = jnp.concatenate([kn_ref[...], kp_ref[...]], axis=1)
        n_chunks = t // rc

        def keys_of(c):
            return (c + 1) * rc if diagonal else t

        def scores(c):
            s = lax.dot_general(q_ref[pl.ds(c * rc, rc), :], k[:keys_of(c)], (((1,), (1,)), ((), ())),
                                preferred_element_type=F32)
            if diagonal:
                q_pos = c * rc + lax.broadcasted_iota(jnp.int32, s.shape, 0)
                k_pos = lax.broadcasted_iota(jnp.int32, s.shape, 1)
                s = jnp.where(k_pos <= q_pos, s, NEG_INF)
            return s

        s_next = scores(0)
        for c in range(n_chunks):
            rows = pl.ds(c * rc, rc)
            s = s_next
            if c + 1 < n_chunks:
                s_next = scores(c + 1)
            m, l, acc = _softmax_step(s, v_ref[0:keys_of(c), :], m_s[rows, :], l_s[rows, :], acc_s[rows, :])
            m_s[rows, :] = m
            l_s[rows, :] = l
            acc_s[rows, :] = acc

    @pl.when(ki == qi)
    def _():
        block(True)
        o_ref[...] = (acc_s[...] / l_s[...]).astype(o_ref.dtype)

    @pl.when(ki != qi)
    def _():
        block(False)


def _flash(q_cat, kv_up, kpe, *, bsz, s_len, t):
    tq = tk = t
    nq = nkb = s_len // t
    pairs = [(qb, kb) for qb in range(nq) for kb in range(qb + 1)]
    qidx = jnp.asarray([p[0] for p in pairs], jnp.int32)
    kidx = jnp.asarray([p[1] for p in pairs], jnp.int32)
    hd = QK_NOPE
    grid_spec = pltpu.PrefetchScalarGridSpec(
        num_scalar_prefetch=2, grid=(bsz, N_HEADS, len(pairs)),
        in_specs=[
            pl.BlockSpec((tq, 2 * hd), lambda b, h, p, qi, ki: (b * nq + qi[p], h)),
            pl.BlockSpec((tk, hd), lambda b, h, p, qi, ki: (b * nkb + ki[p], h)),
            pl.BlockSpec((tk, hd), lambda b, h, p, qi, ki: (b * nkb + ki[p], 0)),
            pl.BlockSpec((tk, hd), lambda b, h, p, qi, ki: (b * nkb + ki[p], N_HEADS + h)),
        ],
        out_specs=pl.BlockSpec((tq, hd), lambda b, h, p, qi, ki: (b * nq + qi[p], h)),
        scratch_shapes=[pltpu.VMEM((tq, 1), F32), pltpu.VMEM((tq, 1), F32), pltpu.VMEM((tq, hd), F32)])
    return pl.pallas_call(functools.partial(_flash_kernel, t=t, rc=min(FLASH_ROW_CHUNK, t)), grid_spec=grid_spec,
                          out_shape=jax.ShapeDtypeStruct((bsz * s_len, N_HEADS * hd), BF16),
                          compiler_params=_cparams("parallel", "parallel", "arbitrary"),
                          name="flash")(qidx, kidx, q_cat, kv_up, kpe, kv_up)


def _paged_kernel(pt_ref, ql_ref, qp_ref, cn_ref, kn_ref, ckv_hbm, kpe_hbm, o_ref,
                  ck_buf, kp_buf, sems, m_s, l_s, acc_s, *, nb, pp, cp, n_pages, t_new):
    n_steps = n_pages // pp
    s_id = pl.program_id(1)
    step = pl.program_id(0) * n_steps + s_id
    last_step = pl.num_programs(0) * n_steps - 1
    slot = lax.rem(step, PAGED_SLOTS)
    nt = (((1,), (1,)), ((), ()))

    def page_copies(for_step, for_slot):
        first_b = lax.div(for_step, n_steps) * nb
        first_page = lax.rem(for_step, n_steps) * pp
        copies = []
        for bi in range(nb):
            for i in range(pp):
                pg = pt_ref[(first_b + bi) * n_pages + first_page + i]
                dst = bi * pp + i
                copies.append(pltpu.make_async_copy(ckv_hbm.at[pg], ck_buf.at[for_slot, dst], sems.at[0, for_slot]))
                copies.append(pltpu.make_async_copy(kpe_hbm.at[pg], kp_buf.at[for_slot, dst], sems.at[1, for_slot]))
        return copies

    @pl.when(step == 0)
    def _():
        for dma in page_copies(0, 0):
            dma.start()
        for dma in page_copies(jnp.minimum(1, last_step), 1):
            dma.start()

    @pl.when(s_id == 0)
    def _():
        m_s[...] = jnp.full_like(m_s, NEG_INF)
        l_s[...] = jnp.zeros_like(l_s)
        acc_s[...] = jnp.zeros_like(acc_s)

    for dma in page_copies(step, slot):
        dma.wait()

    def scores(c, bi):
        first = bi * pp + c * cp
        ck = ck_buf[slot, first:first + cp].reshape(cp * PAGE, ck_buf.shape[-1]).astype(BF16)
        kp_t = jnp.concatenate([kp_buf[slot, first + i] for i in range(cp)], axis=1).astype(BF16)
        s = (lax.dot_general(ql_ref[bi], ck, nt, preferred_element_type=F32)
             + jnp.dot(qp_ref[bi], kp_t, preferred_element_type=F32))
        return s, ck

    items = [(c, bi) for c in range(pp // cp) for bi in range(nb)]
    state = [(m_s[bi], l_s[bi], acc_s[bi]) for bi in range(nb)]
    ahead = {i: scores(*items[i]) for i in range(min(nb, len(items)))}
    for i, (c, bi) in enumerate(items):
        if i + nb < len(items):
            ahead[i + nb] = scores(*items[i + nb])
        s, ck = ahead.pop(i)
        state[bi] = _softmax_step(s, ck, *state[bi])
    for bi in range(nb):
        m_s[bi], l_s[bi], acc_s[bi] = state[bi]

    for dma in page_copies(jnp.minimum(step + 2, last_step), lax.rem(step + 2, PAGED_SLOTS)):
        dma.start()

    @pl.when(s_id == n_steps - 1)
    def _():
        for bi in range(nb):
            cn = cn_ref[bi]
            s_new = (lax.dot_general(ql_ref[bi], cn, nt, preferred_element_type=F32)
                     + lax.dot_general(qp_ref[bi], kn_ref[bi], nt, preferred_element_type=F32))
            tok = lax.broadcasted_iota(jnp.int32, s_new.shape, 0) % t_new
            key = lax.broadcasted_iota(jnp.int32, s_new.shape, 1)
            s_new = jnp.where(key <= tok, s_new, NEG_INF)
            _, l, acc = _softmax_step(s_new, cn, m_s[bi], l_s[bi], acc_s[bi])
            o_ref[bi] = (acc / l).astype(o_ref.dtype)

    @pl.when(step == last_step)
    def _():
        for ahead_by in (1, 2):
            for dma in page_copies(last_step, lax.rem(step + ahead_by, PAGED_SLOTS)):
                dma.wait()


def _paged_attention(q_lat, q_pe, c_new, k_new, cache_ckv, cache_kpe_t, page_table, *, nb, pp):
    bsz, rows, c_dim = q_lat.shape
    r_dim = q_pe.shape[-1]
    n_pages = page_table.shape[1]
    t_new = rows // N_HEADS
    kpad = c_new.shape[1]
    qmap = lambda b, s, pt: (b, 0, 0)
    any_spec = pl.BlockSpec(memory_space=pl.ANY)
    in_specs = [pl.BlockSpec((nb, rows, c_dim), qmap), pl.BlockSpec((nb, rows, r_dim), qmap),
                pl.BlockSpec((nb, kpad, c_dim), qmap), pl.BlockSpec((nb, kpad, r_dim), qmap),
                any_spec, any_spec]
    grid_spec = pltpu.PrefetchScalarGridSpec(
        num_scalar_prefetch=1, grid=(bsz // nb, n_pages // pp), in_specs=in_specs,
        out_specs=pl.BlockSpec((nb, rows, c_dim), qmap),
        scratch_shapes=[pltpu.VMEM((PAGED_SLOTS, nb * pp, PAGE, c_dim), F32),
                        pltpu.VMEM((PAGED_SLOTS, nb * pp, r_dim, PAGE), F32),
                        pltpu.SemaphoreType.DMA((2, PAGED_SLOTS)),
                        pltpu.VMEM((nb, rows, 1), F32), pltpu.VMEM((nb, rows, 1), F32),
                        pltpu.VMEM((nb, rows, c_dim), F32)])
    kern = functools.partial(_paged_kernel, nb=nb, pp=pp, cp=min(PAGED_KEY_CHUNK_PAGES, pp), n_pages=n_pages,
                             t_new=t_new)
    return pl.pallas_call(kern, grid_spec=grid_spec,
                          out_shape=jax.ShapeDtypeStruct((bsz, rows, c_dim), BF16),
                          compiler_params=_cparams("arbitrary", "arbitrary"), name="paged")(
        page_table.reshape(-1), q_lat, q_pe, c_new, k_new, cache_ckv, cache_kpe_t)


def _top2_kernel(lg_ref, idx_ref, gate_ref):
    lg = lg_ref[...]
    col = lax.broadcasted_iota(jnp.int32, lg.shape, 1).astype(F32)
    lg = jnp.where(col < N_EXPERTS, lg, NEG_INF)
    m1 = jnp.max(lg, axis=1, keepdims=True)
    i1 = jnp.min(jnp.where(lg == m1, col, float(LANES)), axis=1, keepdims=True)
    lg2 = jnp.where(col == i1, NEG_INF, lg)
    m2 = jnp.max(lg2, axis=1, keepdims=True)
    i2 = jnp.min(jnp.where(lg2 == m2, col, float(LANES)), axis=1, keepdims=True)
    e2 = jnp.exp(m2 - m1)
    den = 1.0 + e2
    idx_ref[...] = jnp.where(col == 0.0, i1, i2).astype(jnp.int32)
    gate_ref[...] = jnp.where(col == 0.0, 1.0 / den, e2 / den)


def _top2(logits):
    m = logits.shape[0]
    tm = _pick(m, (1024, 512, 256, 128, 64, 32, 16, 8))
    spec = pl.BlockSpec((tm, LANES), lambda i: (i, 0))
    return pl.pallas_call(_top2_kernel, grid=(m // tm,), in_specs=[spec], out_specs=[spec, spec],
                          out_shape=[jax.ShapeDtypeStruct((m, LANES), jnp.int32),
                                     jax.ShapeDtypeStruct((m, LANES), F32)],
                          compiler_params=_cparams("parallel"), name="top2")(logits)


ROW_COPY_UNROLL = 8


def _gather_into(src_hbm, dst_vmem, sem, n, src_row, dst_row):
    def copy(s_row, d_row):
        return pltpu.make_async_copy(src_hbm.at[pl.ds(s_row, 1), :], dst_vmem.at[pl.ds(d_row, 1), :], sem)

    assert n % ROW_COPY_UNROLL == 0

    def issue(g, carry):
        for u in range(ROW_COPY_UNROLL):
            r = g * ROW_COPY_UNROLL + u
            copy(src_row(r), dst_row(r)).start(priority=u % 2)
        return carry

    lax.fori_loop(0, n // ROW_COPY_UNROLL, issue, 0)

    def drain(g, carry):
        for _ in range(ROW_COPY_UNROLL):
            copy(0, 0).wait()
        return carry

    lax.fori_loop(0, n // ROW_COPY_UNROLL, drain, 0)


def _gather_rows_kernel(idx_ref, src_ref, o_ref, sem, *, tm):
    base = pl.program_id(0) * tm
    _gather_into(src_ref, o_ref, sem, tm, lambda r: idx_ref[base + r], lambda r: r)


def _gather_rows(src, idx, *, tm):
    n = idx.shape[0]
    d = src.shape[1]
    grid_spec = pltpu.PrefetchScalarGridSpec(
        num_scalar_prefetch=1, grid=(n // tm,), in_specs=[pl.BlockSpec(memory_space=pl.ANY)],
        out_specs=pl.BlockSpec((tm, d), lambda t, idx: (t, 0)), scratch_shapes=[pltpu.SemaphoreType.DMA(())])
    return pl.pallas_call(functools.partial(_gather_rows_kernel, tm=tm), grid_spec=grid_spec,
                          out_shape=jax.ShapeDtypeStruct((n, d), src.dtype),
                          compiler_params=_cparams("arbitrary"), name="gather_rows")(idx, src)


def _combine_kernel(pos_ref, h_ref, gt_ref, gain_ref, ys_ref, hs_ref, hn_ref, g_s, sem, *, tm):
    base = pl.program_id(0) * tm * TOP_K
    _gather_into(ys_ref, g_s, sem, tm * TOP_K, lambda a: pos_ref[base + a],
                 lambda a: lax.rem(a, TOP_K) * tm + lax.div(a, TOP_K))
    gt = gt_ref[...]
    y = gt[:, 0:1] * g_s[0:tm, :] + gt[:, 1:2] * g_s[tm:2 * tm, :]
    hs = h_ref[...] + y
    hs_ref[...] = hs
    yn = hs * lax.rsqrt(jnp.mean(hs * hs, axis=-1, keepdims=True) + EPS)
    hn_ref[...] = (yn * gain_ref[...]).astype(BF16)


def _combine(h, ys, pos, gates, gain, *, tm):
    m, d = h.shape
    row = lambda w: pl.BlockSpec((tm, w), lambda i, pos: (i, 0))
    grid_spec = pltpu.PrefetchScalarGridSpec(
        num_scalar_prefetch=1, grid=(m // tm,),
        in_specs=[row(d), row(LANES), pl.BlockSpec((1, d), lambda i, pos: (0, 0)), pl.BlockSpec(memory_space=pl.ANY)],
        out_specs=[row(d), row(d)],
        scratch_shapes=[pltpu.VMEM((TOP_K * tm, d), F32), pltpu.SemaphoreType.DMA(())])
    return pl.pallas_call(functools.partial(_combine_kernel, tm=tm), grid_spec=grid_spec,
                          out_shape=[jax.ShapeDtypeStruct((m, d), F32), jax.ShapeDtypeStruct((m, d), BF16)],
                          compiler_params=_cparams("arbitrary"), name="combine")(
        pos, h, gates, gain.reshape(1, d), ys)


MOE_TILE = 1024
MOE_SUB = 256
MOE_TF = 512
FFN_TILE = 1024
LRU_CHUNK = 512
FLASH_TILE = 1024
FLASH_ROW_CHUNK = 256
PAGED_KEY_CHUNK_PAGES = 4
PAGES_PER_STEP = 16
COMBINE_TILE = 256
PAGED_BATCH = 2
PAGED_SLOTS = 3


def _moe(xn_f32, router, w_gate, w_up, w_down):
    m, d = xn_f32.shape
    n_exp = router.shape[-1]
    router_p = jnp.pad(router, ((0, 0), (0, LANES - n_exp)))
    logits = _mm(xn_f32, router_p, tn=LANES)
    idx, gates = _top2(logits)
    flat_e = idx[:, :TOP_K].reshape(-1)
    n_assign = flat_e.shape[0]
    onehot = (flat_e[:, None] == jnp.arange(n_exp, dtype=jnp.int32)[None, :]).astype(jnp.int32)
    csum = jnp.cumsum(onehot, axis=0)
    rank = jnp.take_along_axis(csum, flat_e[:, None], axis=1)[:, 0] - 1
    counts = csum[-1]
    tiles_e = (counts + MOE_TILE - 1) // MOE_TILE
    tile_end = jnp.cumsum(tiles_e)
    tile_start = tile_end - tiles_e
    pos = (tile_start[flat_e] * MOE_TILE + rank).astype(jnp.int32)
    n_tiles = n_assign // MOE_TILE + n_exp
    tid = jnp.arange(n_tiles, dtype=jnp.int32)
    te = jnp.minimum(jnp.sum((tile_end[None, :] <= tid[:, None]).astype(jnp.int32), axis=1), n_exp - 1)
    active = tid < tile_end[-1]
    last_e = jnp.max(jnp.where(counts > 0, jnp.arange(n_exp, dtype=jnp.int32), 0))
    te = jnp.where(active, te, last_e).astype(jnp.int32)
    rows_left = counts[te] - (tid - tile_start[te]) * MOE_TILE
    nsub = jnp.where(active, (jnp.clip(rows_left, 0, MOE_TILE) + MOE_SUB - 1) // MOE_SUB, 0).astype(jnp.int32)

    p_rows = n_tiles * MOE_TILE
    tok = (jnp.arange(n_assign, dtype=jnp.int32) // TOP_K).astype(jnp.int32)
    slot_tok = (jnp.arange(p_rows, dtype=jnp.int32) % m).at[pos].set(tok)
    xs = _gather_rows(xn_f32, slot_tok, tm=MOE_TILE)
    ys = _ffn(xs, w_gate, w_up, w_down, te, nsub, tm=MOE_TILE, tf=MOE_TF, sub=MOE_SUB)
    return ys, pos, gates


def _rope_tables(pos):
    inv = 1.0 / (ROPE_THETA ** (jnp.arange(0, QK_ROPE, 2, dtype=F32) / QK_ROPE))
    ang = pos[:, None] * inv[None, :]
    return jnp.cos(ang), jnp.sin(ang)


def _rot_cols(w):
    half = w.shape[-1] // 2
    return jnp.concatenate([-w[..., half:], w[..., :half]], axis=-1)


def kernel(x_prompt, x_sample, p_prompt, p_sample, cache_ckv, cache_kpe, page_table, state_h, state_conv, norm_mix, norm_ffn, norm_ple, norm_final, lru_w_in, lru_conv_w, lru_conv_b, lru_w_a, lru_b_a, lru_w_i, lru_b_i, lru_lambda, lru_w_out, mla_wq_a, mla_q_norm, mla_wq_b, mla_wkv_a, mla_kv_norm, mla_wkv_b, mla_wo, ffn_w_gate, ffn_w_up, ffn_w_down, moe_router, moe_w_gate, moe_w_up, moe_w_down, ple_w_gate, ple_b_gate, ple_w_proj):
    bp, sp, d = x_prompt.shape
    bs, ts, _ = x_sample.shape
    n_p, n_s = bp * sp, bs * ts
    w_lru = lru_w_out.shape[1]
    past_len = page_table.shape[1] * PAGE

    h = jnp.concatenate([x_prompt.reshape(n_p, d), x_sample.reshape(n_s, d)], axis=0)
    ple_in = jnp.concatenate([p_prompt.reshape(p_prompt.shape[0], n_p, -1),
                              p_sample.reshape(p_sample.shape[0], n_s, -1)], axis=1)

    def ple(h, hn, i):
        return _mm(hn, ple_w_gate[i], bias=ple_b_gate[i], act="sigmoid", mul=(ple_in[i], ple_w_proj[i]), res=h)

    (xn,) = _resnorm(h, norm_mix[0])
    proj = _mm(xn, lru_w_in[0])
    proj_s = proj[n_p:].reshape(bs, ts, 2 * w_lru)
    lw = (lru_conv_w[0], lru_conv_b[0], lru_w_a[0], lru_b_a[0], lru_w_i[0], lru_b_i[0], lru_lambda[0])
    hy_p, hl_p = _lru(proj, *lw, bsz=bp, t_len=sp, chunk=min(LRU_CHUNK, sp))
    zpad = jnp.zeros((bs, 2 * SUBLANES - ts - (CONV_WIDTH - 1), w_lru), F32)
    xb_s = jnp.concatenate([zpad, state_conv[0], proj_s[..., :w_lru]], axis=1)
    yb_s = jnp.concatenate([jnp.zeros((bs, 2 * SUBLANES - ts, w_lru), F32), proj_s[..., w_lru:]], axis=1)
    h0_rows = jnp.zeros((bs, 2 * SUBLANES, w_lru), F32).at[:, 2 * SUBLANES - ts - 1].set(state_h[0])
    seg_rows = bs * 2 * SUBLANES
    hy_s, hf_s = _lru(jnp.concatenate([xb_s, yb_s], axis=-1).reshape(seg_rows, 2 * w_lru), *lw,
                      bsz=1, t_len=seg_rows, h0_rows=h0_rows.reshape(seg_rows, w_lru))
    hy_s = hy_s.reshape(bs, 2 * SUBLANES, w_lru)[:, 2 * SUBLANES - ts:]
    hf_s = hf_s.reshape(bs, 2 * SUBLANES, w_lru)
    hy = jnp.concatenate([hy_p, hy_s.reshape(n_s, w_lru)], axis=0)
    h = _mm(hy, lru_w_out[0], res=h)
    new_h_prompt = hl_p[:, SUBLANES - 1][None]
    new_h_sample = hf_s[:, -1][None]
    new_conv_prompt = jnp.stack([proj[(b + 1) * sp - (CONV_WIDTH - 1):(b + 1) * sp, :w_lru] for b in range(bp)])[None]
    new_conv_sample = xb_s[:, 2 * SUBLANES - (CONV_WIDTH - 1):][None]

    (xn,) = _resnorm(h, norm_ffn[0])
    tm_ffn = _pick(h.shape[0], (FFN_TILE, 512, 256))
    nt_ffn = h.shape[0] // tm_ffn
    f = _ffn(xn, ffn_w_gate, ffn_w_up, ffn_w_down, jnp.zeros((nt_ffn,), jnp.int32),
             jnp.full((nt_ffn,), tm_ffn // MOE_SUB, jnp.int32), tm=tm_ffn, tf=MOE_TF, sub=MOE_SUB)
    h, hn = _resnorm(h, norm_ple[0], add=f, want_sum=True)
    h = ple(h, hn, 0)

    (xn,) = _resnorm(h, norm_mix[1])
    cq = _mm(xn, mla_wq_a[0])
    (cqn,) = _resnorm(cq, mla_q_norm[0])
    pos_all = jnp.concatenate([jnp.tile(jnp.arange(sp, dtype=F32), bp),
                               jnp.tile(jnp.arange(ts, dtype=F32) + past_len, bs)])
    cos, sin = _rope_tables(pos_all)
    zeros_r = jnp.zeros((n_p + n_s, LANES - QK_ROPE), F32)
    cos_k = jnp.concatenate([cos, cos, zeros_r], axis=1)
    sin_k = jnp.concatenate([sin, sin, zeros_r], axis=1)
    ones_n = jnp.ones((n_p + n_s, QK_NOPE), F32)
    cos_q = jnp.tile(jnp.concatenate([ones_n, cos_k], axis=1), (1, 2))
    sin_q = jnp.tile(jnp.concatenate([0.0 * ones_n, sin_k], axis=1), (1, 2))
    wq_b = mla_wq_b[0]
    cq_dim = wq_b.shape[0]
    wq_pe = wq_b[..., QK_NOPE:]
    zq = jnp.zeros((cq_dim, N_HEADS, LANES - QK_ROPE), F32)
    wq_main = jnp.concatenate([wq_b[..., :QK_NOPE], wq_pe, zq], axis=-1).reshape(cq_dim, N_HEADS * 2 * LANES)
    wq_rot = jnp.concatenate([jnp.zeros((cq_dim, N_HEADS, QK_NOPE), F32), _rot_cols(wq_pe), zq],
                             axis=-1).reshape(cq_dim, N_HEADS * 2 * LANES)
    q_cat = _mm(cqn, wq_main, w2=wq_rot, cos=cos_q, sin=sin_q, out_dtype=BF16, tn=4 * LANES)

    wkv_a = mla_wkv_a[0]
    ckv_raw = _mm(xn, wkv_a[:, :KV_LORA])
    ckv_f32, ckv_bf = _resnorm(ckv_raw, mla_kv_norm[0], want_f32=True)
    zk = jnp.zeros((d, LANES - QK_ROPE), F32)
    wk_pe = wkv_a[:, KV_LORA:]
    kpe_pad = _mm(xn, jnp.concatenate([wk_pe, zk], axis=1),
                  w2=jnp.concatenate([_rot_cols(wk_pe), zk], axis=1), cos=cos_k, sin=sin_k, tn=LANES)
    kpe_bf = kpe_pad.astype(BF16)
    new_ckv_prompt = ckv_f32[:n_p].reshape(1, bp, sp, KV_LORA)
    new_ckv_sample = ckv_f32[n_p:].reshape(1, bs, ts, KV_LORA)
    new_kpe_prompt = kpe_pad[:n_p, :QK_ROPE].reshape(1, bp, sp, QK_ROPE)
    new_kpe_sample = kpe_pad[n_p:, :QK_ROPE].reshape(1, bs, ts, QK_ROPE)

    wkv_b = mla_wkv_b[0]
    wkv_up = jnp.concatenate([wkv_b[..., :QK_NOPE].reshape(KV_LORA, -1), wkv_b[..., QK_NOPE:].reshape(KV_LORA, -1)],
                             axis=1)
    kv_up = _mm(ckv_bf, wkv_up, out_dtype=BF16, m=n_p)
    t_att = _pick(sp, (FLASH_TILE, 512, 256, 128))
    o_p = _flash(q_cat, kv_up, kpe_bf, bsz=bp, s_len=sp, t=t_att)

    q_s = q_cat[n_p:].reshape(bs, ts, N_HEADS, 2 * LANES)
    qn_s = jnp.transpose(q_s[..., :QK_NOPE], (2, 0, 1, 3)).reshape(N_HEADS, n_s, QK_NOPE)
    w_uk_t = jnp.transpose(wkv_b[..., :QK_NOPE], (1, 2, 0))
    q_lat = _mm(qn_s, w_uk_t, out_dtype=BF16)
    q_lat = jnp.transpose(q_lat.reshape(N_HEADS, bs, ts, KV_LORA), (1, 0, 2, 3)).reshape(bs, N_HEADS * ts, KV_LORA)
    q_pe = jnp.transpose(q_s[..., QK_NOPE:QK_NOPE + QK_ROPE], (0, 2, 1, 3)).reshape(bs, N_HEADS * ts, QK_ROPE)
    kpad = 2 * SUBLANES
    c_new = jnp.pad(ckv_bf[n_p:].reshape(bs, ts, KV_LORA), ((0, 0), (0, kpad - ts), (0, 0)))
    k_new = jnp.pad(kpe_bf[n_p:, :QK_ROPE].reshape(bs, ts, QK_ROPE), ((0, 0), (0, kpad - ts), (0, 0)))
    o_lat = _paged_attention(q_lat, q_pe, c_new, k_new, cache_ckv[0], jnp.swapaxes(cache_kpe[0], 1, 2), page_table,
                             nb=PAGED_BATCH, pp=min(PAGES_PER_STEP, page_table.shape[1]))
    o_lat = jnp.transpose(o_lat.reshape(bs, N_HEADS, ts, KV_LORA), (1, 0, 2, 3)).reshape(N_HEADS, n_s, KV_LORA)
    w_uv = jnp.transpose(wkv_b[..., QK_NOPE:], (1, 0, 2))
    o_s = _mm(o_lat, w_uv, out_dtype=BF16, tn=V_HEAD)
    o_s = jnp.transpose(o_s, (1, 0, 2)).reshape(n_s, N_HEADS * V_HEAD)
    o_all = jnp.concatenate([o_p, o_s], axis=0)
    h = _mm(o_all, mla_wo[0], res=h)

    (xn_f32,) = _resnorm(h, norm_ffn[1], want_f32=True, want_bf16=False)
    ys, pos, gates = _moe(xn_f32, moe_router[0], moe_w_gate[0], moe_w_up[0], moe_w_down[0])
    h, hn = _combine(h, ys, pos, gates, norm_ple[1], tm=_pick(h.shape[0], (COMBINE_TILE, 128, 64, 32, 16, 8)))
    h = ple(h, hn, 1)

    (y_prompt,) = _resnorm(h, norm_final, want_f32=True, want_bf16=False, rows=(0, n_p))
    (y_sample,) = _resnorm(h, norm_final, want_f32=True, want_bf16=False, rows=(n_p, n_s))
    y_prompt = y_prompt.reshape(bp, sp, d)
    y_sample = y_sample.reshape(bs, ts, d)
    return (y_prompt, y_sample, new_ckv_prompt, new_kpe_prompt, new_ckv_sample, new_kpe_sample,
            new_h_prompt, new_conv_prompt, new_h_sample, new_conv_sample)
```

```python
import functools
import math

import jax
import jax.numpy as jnp
from jax import lax
from jax.experimental import pallas as pl
from jax.experimental.pallas import tpu as pltpu

F32 = jnp.float32
BF16 = jnp.bfloat16

N_HEADS = 16
QK_NOPE = 128
QK_ROPE = 64
V_HEAD = 128
KV_LORA = 512
LRU_HEADS = 8
LRU_C = 8.0
CONV_WIDTH = 4
N_EXPERTS = 8
TOP_K = 2
PAGE = 128
ROPE_THETA = 10000.0
EPS = 1e-6
MLA_SCALE = (QK_NOPE + QK_ROPE) ** -0.5
SCALE_LOG2E = MLA_SCALE * math.log2(math.e)

LANES = 128
SUBLANES = 8
VMEM_LIMIT_BYTES = 56 * 1024 * 1024

NEG_INF = float("-inf")


def _cparams(*sem):
    return pltpu.CompilerParams(dimension_semantics=sem, vmem_limit_bytes=VMEM_LIMIT_BYTES)


def _pick(n, prefs):
    for p in prefs:
        if n % p == 0:
            return p
    return n


def _resnorm_kernel(*refs, has_add, want_sum, want_f32, want_bf16):
    it = iter(refs)
    h_ref = next(it)
    f_ref = next(it) if has_add else None
    gain_ref = next(it)
    hs = h_ref[...]
    if has_add:
        hs = hs + f_ref[...]
    if want_sum:
        next(it)[...] = hs
    y = hs * lax.rsqrt(jnp.mean(hs * hs, axis=-1, keepdims=True) + EPS)
    y = y * gain_ref[...]
    if want_f32:
        next(it)[...] = y
    if want_bf16:
        next(it)[...] = y.astype(BF16)


def _resnorm(h, gain, *, add=None, want_sum=False, want_f32=False, want_bf16=True, tm=None, rows=None):
    d = h.shape[1]
    r0, m = rows if rows is not None else (0, h.shape[0])
    tm = tm or _pick(math.gcd(m, r0) if r0 else m, (512, 256, 128, 64, 32, 16, 8))
    b0 = r0 // tm
    row = pl.BlockSpec((tm, d), lambda i: (i, 0))
    row_in = pl.BlockSpec((tm, d), lambda i: (i + b0, 0))
    ins, specs = [h], [row_in]
    if add is not None:
        ins.append(add)
        specs.append(row_in)
    ins.append(gain.reshape(1, d))
    specs.append(pl.BlockSpec((1, d), lambda i: (0, 0)))
    outs, ospecs = [], []
    if want_sum:
        outs.append(jax.ShapeDtypeStruct((m, d), F32))
        ospecs.append(row)
    if want_f32:
        outs.append(jax.ShapeDtypeStruct((m, d), F32))
        ospecs.append(row)
    if want_bf16:
        outs.append(jax.ShapeDtypeStruct((m, d), BF16))
        ospecs.append(row)
    kern = functools.partial(_resnorm_kernel, has_add=add is not None,
                             want_sum=want_sum, want_f32=want_f32, want_bf16=want_bf16)
    return pl.pallas_call(kern, grid=(m // tm,), in_specs=specs, out_specs=ospecs, out_shape=outs,
                          compiler_params=_cparams("parallel"), name="resnorm")(*ins)


def _mm_kernel(*refs, nk, act, rope, has_bias, has_mul, mul_is_product, has_res):
    it = iter(refs)
    x_ref = next(it)
    w_ref = next(it)
    w2_ref = next(it) if rope else None
    c_ref = next(it) if rope else None
    s_ref = next(it) if rope else None
    b_ref = next(it) if has_bias else None
    m_ref = next(it) if has_mul else None
    m_w_ref = next(it) if mul_is_product else None
    r_ref = next(it) if has_res else None
    o_ref = next(it)
    acc_ref = next(it) if nk > 1 else None

    x = x_ref[...].astype(BF16)
    p = jnp.dot(x, w_ref[...].astype(BF16), preferred_element_type=F32)
    if rope:
        p2 = jnp.dot(x, w2_ref[...].astype(BF16), preferred_element_type=F32)
        p = p * c_ref[...] + p2 * s_ref[...]

    def finish(acc):
        if has_bias:
            acc = acc + b_ref[...]
        if act == "sigmoid":
            acc = jax.nn.sigmoid(acc)
        if mul_is_product:
            acc = acc * jnp.dot(m_ref[...].astype(BF16), m_w_ref[...].astype(BF16), preferred_element_type=F32)
        elif has_mul:
            acc = m_ref[...] * acc
        if has_res:
            acc = r_ref[...] + acc
        o_ref[...] = acc.astype(o_ref.dtype)

    if nk == 1:
        finish(p)
    else:
        k = pl.program_id(3)

        @pl.when(k == 0)
        def _():
            acc_ref[...] = p

        @pl.when(k > 0)
        def _():
            acc_ref[...] += p

        @pl.when(k == nk - 1)
        def _():
            finish(acc_ref[...])


def _mm(x, w, *, w2=None, cos=None, sin=None, bias=None, act=None, mul=None, res=None,
        out_dtype=F32, tm=None, tn=None, tk=None, m=None):
    squeeze = x.ndim == 2
    mul_is_product = isinstance(mul, tuple)
    if squeeze:
        x, w = x[None], w[None]
        w2 = None if w2 is None else w2[None]
        cos = None if cos is None else cos[None]
        sin = None if sin is None else sin[None]
        bias = None if bias is None else bias.reshape(1, 1, -1)
        if mul_is_product:
            mul = (mul[0][None], mul[1][None])
        elif mul is not None:
            mul = mul[None]
        res = None if res is None else res[None]
    g, m_all, kd = x.shape
    m = m or m_all
    n = w.shape[-1]
    tm = tm or _pick(m, (1024, 512, 256, 128, 64, 32, 16, 8))
    tn = tn or _pick(n, (1024, 512, 256, 128) if mul is None else (512, 256, 128))
    tk = tk or (kd if kd <= 2048 else _pick(kd, (1024, 512)))
    nk = kd // tk
    rope = w2 is not None
    xs = pl.BlockSpec((None, tm, tk), lambda gi, i, j, k: (gi, i, k))
    ws = pl.BlockSpec((None, tk, tn), lambda gi, i, j, k: (gi, k, j))
    os_ = pl.BlockSpec((None, tm, tn), lambda gi, i, j, k: (gi, i, j))
    ins, specs = [x, w], [xs, ws]
    if rope:
        ncb = cos.shape[-1] // tn
        tab = pl.BlockSpec((None, tm, tn), lambda gi, i, j, k: (gi, i, j % ncb))
        ins += [w2, cos, sin]
        specs += [ws, tab, tab]
    if bias is not None:
        ins.append(bias)
        specs.append(pl.BlockSpec((None, 1, tn), lambda gi, i, j, k: (gi, 0, j)))
    if mul_is_product:
        assert nk == 1
        k2 = mul[0].shape[-1]
        ins += [mul[0], mul[1]]
        specs += [pl.BlockSpec((None, tm, k2), lambda gi, i, j, k: (gi, i, 0)),
                  pl.BlockSpec((None, k2, tn), lambda gi, i, j, k: (gi, 0, j))]
    elif mul is not None:
        ins.append(mul)
        specs.append(os_)
    if res is not None:
        ins.append(res)
        specs.append(os_)
    kern = functools.partial(_mm_kernel, nk=nk, act=act, rope=rope, has_bias=bias is not None,
                             has_mul=mul is not None, mul_is_product=mul_is_product, has_res=res is not None)
    scratch = [pltpu.VMEM((tm, tn), F32)] if nk > 1 else []
    out = pl.pallas_call(kern, grid=(g, m // tm, n // tn, nk), in_specs=specs, out_specs=os_,
                         out_shape=jax.ShapeDtypeStruct((g, m, n), out_dtype), scratch_shapes=scratch,
                         compiler_params=_cparams("parallel", "parallel", "parallel", "arbitrary"),
                         name="mm")(*ins)
    return out[0] if squeeze else out


def _ffn_kernel(te_ref, ns_ref, x_ref, wg_ref, wu_ref, wd_ref, o_ref, wg_s, wu_s, wd_s, *, sub):
    t = pl.program_id(0)
    j = pl.program_id(1)
    ns = ns_ref[t]

    @pl.when(j == 0)
    def _():
        o_ref[...] = jnp.zeros_like(o_ref)

    def block(r, rows, wg, wu, wd):
        xs = x_ref[pl.ds(r, rows), :].astype(BF16)
        gate = jnp.dot(xs, wg, preferred_element_type=F32)
        up = jnp.dot(xs, wu, preferred_element_type=F32)
        mid = (jax.nn.silu(gate) * up).astype(BF16)
        o_ref[pl.ds(r, rows), :] += jnp.dot(mid, wd, preferred_element_type=F32)

    def first_block(rows):
        wg = wg_ref[...].astype(BF16)
        wu = wu_ref[...].astype(BF16)
        wd = wd_ref[...].astype(BF16)
        wg_s[...] = wg
        wu_s[...] = wu
        wd_s[...] = wd
        block(0, rows, wg, wu, wd)

    @pl.when(ns == 1)
    def _():
        first_block(sub)

    @pl.when(ns >= 2)
    def _():
        first_block(2 * sub)

    @pl.when(ns > 2)
    def _():
        rest = ns - 2
        n_pairs = lax.shift_right_logical(rest, 1)

        def pair(p, carry):
            block(pl.multiple_of((p + 1) * (2 * sub), sub), 2 * sub, wg_s[...], wu_s[...], wd_s[...])
            return carry

        lax.fori_loop(0, n_pairs, pair, 0)

        @pl.when((rest & 1) == 1)
        def _():
            block(pl.multiple_of((n_pairs + 1) * (2 * sub), sub), sub, wg_s[...], wu_s[...], wd_s[...])


def _ffn(x, w_gate, w_up, w_down, tile_expert, tile_nsub, *, tm, tf, sub):
    p_rows, d = x.shape
    f = w_gate.shape[-1]
    nt, nj = p_rows // tm, f // tf

    def jj(t, j, ns):
        return jnp.where(ns[t] > 0, j, nj - 1)

    once = pl.Buffered(1)
    grid_spec = pltpu.PrefetchScalarGridSpec(
        num_scalar_prefetch=2, grid=(nt, nj),
        in_specs=[
            pl.BlockSpec((tm, d), lambda t, j, te, ns: (t, 0), pipeline_mode=once),
            pl.BlockSpec((None, d, tf), lambda t, j, te, ns: (te[t], 0, jj(t, j, ns))),
            pl.BlockSpec((None, d, tf), lambda t, j, te, ns: (te[t], 0, jj(t, j, ns))),
            pl.BlockSpec((None, tf, d), lambda t, j, te, ns: (te[t], jj(t, j, ns), 0)),
        ],
        out_specs=pl.BlockSpec((tm, d), lambda t, j, te, ns: (t, 0), pipeline_mode=once),
        scratch_shapes=[pltpu.VMEM((d, tf), BF16), pltpu.VMEM((d, tf), BF16), pltpu.VMEM((tf, d), BF16)])
    return pl.pallas_call(functools.partial(_ffn_kernel, sub=sub), grid_spec=grid_spec,
                          out_shape=jax.ShapeDtypeStruct((p_rows, d), F32),
                          compiler_params=_cparams("parallel", "arbitrary"),
                          name="ffn")(tile_expert, tile_nsub, x, w_gate, w_up, w_down)


def _softplus(x):
    return jnp.maximum(x, 0.0) + jnp.log1p(jnp.exp(-jnp.abs(x)))


def _lru_kernel(*refs, seg, carry, pre, rows):
    it = iter(refs)
    xb_ref = next(it)
    yb_ref = next(it)
    cw_ref = next(it)
    cb_ref = next(it)
    wa_ref = next(it)
    ba_ref = next(it)
    wi_ref = next(it)
    bi_ref = next(it)
    lam_ref = next(it)
    h0_ref = next(it) if pre else None
    hy_ref = next(it)
    h_ref = next(it)
    xpad_s = next(it)
    hc_s = next(it)

    c = pl.program_id(2)
    if carry:
        @pl.when(c == 0)
        def _():
            xpad_s[0:SUBLANES, :] = jnp.zeros((SUBLANES, xpad_s.shape[1]), F32)
            hc_s[...] = jnp.zeros_like(hc_s)
    else:
        xpad_s[0:SUBLANES, :] = jnp.zeros((SUBLANES, xpad_s.shape[1]), F32)

    x = xb_ref[...]
    xpad_s[pl.ds(SUBLANES, rows), :] = x
    cw = cw_ref[...]
    xc = cb_ref[...]
    for k in range(CONV_WIDTH):
        shift = CONV_WIDTH - 1 - k
        xk = x if shift == 0 else xpad_s[pl.ds(SUBLANES - shift, rows), :]
        xc = xc + xk * cw[k:k + 1, :]
    if carry:
        xpad_s[0:SUBLANES, :] = x[rows - SUBLANES:, :]

    xcb = xc.astype(BF16)
    ga = jnp.dot(xcb, wa_ref[...].astype(BF16), preferred_element_type=F32) + ba_ref[...]
    gi = jnp.dot(xcb, wi_ref[...].astype(BF16), preferred_element_type=F32) + bi_ref[...]
    r = jax.nn.sigmoid(ga)
    ig = jax.nn.sigmoid(gi)
    log_a = (-LRU_C * r) * _softplus(-lam_ref[...])
    a = jnp.exp(log_a)
    mult = jnp.sqrt(-jnp.tanh(log_a) * (a * a + 1.0))
    bx = mult * ig * xc

    row = lax.broadcasted_iota(jnp.int32, a.shape, 0)
    rmod = row & (seg - 1)
    if pre:
        valid = rmod >= seg // 2
        a = jnp.where(valid, a, 0.0)
        bx = jnp.where(valid, bx, h0_ref[...])

    av, bv = a, bx
    dist = 1
    while dist < seg:
        keep = rmod >= dist
        a_sh = jnp.where(keep, pltpu.roll(av, dist, 0), 1.0)
        b_sh = jnp.where(keep, pltpu.roll(bv, dist, 0), 0.0)
        bv = av * b_sh + bv
        av = av * a_sh
        dist *= 2
    if carry:
        h = av * hc_s[SUBLANES - 1:SUBLANES, :] + bv
        hc_s[...] = h[rows - SUBLANES:, :]
    else:
        h = bv
    hy_ref[...] = (h * jax.nn.gelu(yb_ref[...], approximate=True)).astype(BF16)
    if carry:
        h_ref[...] = h[rows - SUBLANES:, :]
    else:
        h_ref[...] = h


def _lru(proj, conv_w, conv_b, w_a, b_a, w_i, b_i, lam, *, bsz, t_len, h0_rows=None, chunk=None):
    w = proj.shape[1] // 2
    hb = w // LRU_HEADS
    pre = h0_rows is not None
    carry = not pre
    rows = chunk if carry else t_len
    seg = chunk if carry else 2 * SUBLANES
    nc = t_len // rows
    blk = lambda off: pl.BlockSpec((rows, hb), lambda b, h, c: (b * nc + c, h + off))
    vec = pl.BlockSpec((1, hb), lambda b, h, c: (0, h))
    mat = pl.BlockSpec((None, hb, hb), lambda b, h, c: (h, 0, 0))
    hvec = pl.BlockSpec((None, 1, hb), lambda b, h, c: (h, 0, 0))
    ins = [proj, proj, conv_w, conv_b.reshape(1, w), w_a, b_a.reshape(LRU_HEADS, 1, hb),
           w_i, b_i.reshape(LRU_HEADS, 1, hb), lam.reshape(1, w)]
    specs = [blk(0), blk(LRU_HEADS), pl.BlockSpec((CONV_WIDTH, hb), lambda b, h, c: (0, h)), vec,
             mat, hvec, mat, hvec, vec]
    if pre:
        ins.append(h0_rows)
        specs.append(blk(0))
    h_rows = SUBLANES if carry else t_len
    outs = [jax.ShapeDtypeStruct((bsz * t_len, w), BF16), jax.ShapeDtypeStruct((bsz, h_rows, w), F32)]
    ospecs = [blk(0), pl.BlockSpec((None, h_rows, hb), lambda b, h, c: (b, 0, h))]
    kern = functools.partial(_lru_kernel, seg=seg, carry=carry, pre=pre, rows=rows)
    return pl.pallas_call(kern, grid=(bsz, LRU_HEADS, nc), in_specs=specs, out_specs=ospecs, out_shape=outs,
                          scratch_shapes=[pltpu.VMEM((rows + SUBLANES, hb), F32), pltpu.VMEM((SUBLANES, hb), F32)],
                          compiler_params=_cparams("parallel", "parallel", "arbitrary"), name="lru")(*ins)


def _softmax_step(s, vals, m, l, acc):
    m_new = jnp.maximum(m, jnp.max(s, axis=1, keepdims=True))
    alpha = jnp.exp2((m - m_new) * SCALE_LOG2E)
    p = jnp.exp2((s - m_new) * SCALE_LOG2E)
    l = alpha * l + jnp.sum(p, axis=1, keepdims=True)
    acc = alpha * acc + jnp.dot(p.astype(BF16), vals, preferred_element_type=F32)
    return m_new, l, acc


def _flash_kernel(qi_ref, ki_ref, q_ref, kn_ref, kp_ref, v_ref, o_ref, m_s, l_s, acc_s, *, t, rc):
    p_id = pl.program_id(2)
    qi = qi_ref[p_id]
    ki = ki_ref[p_id]

    @pl.when(ki == 0)
    def _():
        m_s[...] = jnp.full_like(m_s, NEG_INF)
        l_s[...] = jnp.zeros_like(l_s)
        acc_s[...] = jnp.zeros_like(acc_s)

    def block(diagonal):
        k = jnp.concatenate([kn_ref[...], kp_ref[...]], axis=1)
        n_chunks = t // rc

        def keys_of(c):
            return (c + 1) * rc if diagonal else t

        def scores(c):
            s = lax.dot_general(q_ref[pl.ds(c * rc, rc), :], k[:keys_of(c)], (((1,), (1,)), ((), ())),
                                preferred_element_type=F32)
            if diagonal:
                q_pos = c * rc + lax.broadcasted_iota(jnp.int32, s.shape, 0)
                k_pos = lax.broadcasted_iota(jnp.int32, s.shape, 1)
                s = jnp.where(k_pos <= q_pos, s, NEG_INF)
            return s

        s_next = scores(0)
        for c in range(n_chunks):
            rows = pl.ds(c * rc, rc)
            s = s_next
            if c + 1 < n_chunks:
                s_next = scores(c + 1)
            m, l, acc = _softmax_step(s, v_ref[0:keys_of(c), :], m_s[rows, :], l_s[rows, :], acc_s[rows, :])
            m_s[rows, :] = m
            l_s[rows, :] = l
            acc_s[rows, :] = acc

    @pl.when(ki == qi)
    def _():
        block(True)
        o_ref[...] = (acc_s[...] / l_s[...]).astype(o_ref.dtype)

    @pl.when(ki != qi)
    def _():
        block(False)


def _flash(q_cat, kv_up, kpe, *, bsz, s_len, t):
    tq = tk = t
    nq = nkb = s_len // t
    pairs = [(qb, kb) for qb in range(nq) for kb in range(qb + 1)]
    qidx = jnp.asarray([p[0] for p in pairs], jnp.int32)
    kidx = jnp.asarray([p[1] for p in pairs], jnp.int32)
    hd = QK_NOPE
    grid_spec = pltpu.PrefetchScalarGridSpec(
        num_scalar_prefetch=2, grid=(bsz, N_HEADS, len(pairs)),
        in_specs=[
            pl.BlockSpec((tq, 2 * hd), lambda b, h, p, qi, ki: (b * nq + qi[p], h)),
            pl.BlockSpec((tk, hd), lambda b, h, p, qi, ki: (b * nkb + ki[p], h)),
            pl.BlockSpec((tk, hd), lambda b, h, p, qi, ki: (b * nkb + ki[p], 0)),
            pl.BlockSpec((tk, hd), lambda b, h, p, qi, ki: (b * nkb + ki[p], N_HEADS + h)),
        ],
        out_specs=pl.BlockSpec((tq, hd), lambda b, h, p, qi, ki: (b * nq + qi[p], h)),
        scratch_shapes=[pltpu.VMEM((tq, 1), F32), pltpu.VMEM((tq, 1), F32), pltpu.VMEM((tq, hd), F32)])
    return pl.pallas_call(functools.partial(_flash_kernel, t=t, rc=min(FLASH_ROW_CHUNK, t)), grid_spec=grid_spec,
                          out_shape=jax.ShapeDtypeStruct((bsz * s_len, N_HEADS * hd), BF16),
                          compiler_params=_cparams("parallel", "parallel", "arbitrary"),
                          name="flash")(qidx, kidx, q_cat, kv_up, kpe, kv_up)


def _paged_kernel(pt_ref, ql_ref, qp_ref, cn_ref, kn_ref, ckv_hbm, kpe_hbm, o_ref,
                  ck_buf, kp_buf, sems, m_s, l_s, acc_s, *, nb, pp, cp, n_pages, t_new):
    n_steps = n_pages // pp
    s_id = pl.program_id(1)
    step = pl.program_id(0) * n_steps + s_id
    last_step = pl.num_programs(0) * n_steps - 1
    slot = lax.rem(step, PAGED_SLOTS)
    nt = (((1,), (1,)), ((), ()))

    def page_copies(for_step, for_slot):
        first_b = lax.div(for_step, n_steps) * nb
        first_page = lax.rem(for_step, n_steps) * pp
        copies = []
        for bi in range(nb):
            for i in range(pp):
                pg = pt_ref[(first_b + bi) * n_pages + first_page + i]
                dst = bi * pp + i
                copies.append(pltpu.make_async_copy(ckv_hbm.at[pg], ck_buf.at[for_slot, dst], sems.at[0, for_slot]))
                copies.append(pltpu.make_async_copy(kpe_hbm.at[pg], kp_buf.at[for_slot, dst], sems.at[1, for_slot]))
        return copies

    @pl.when(step == 0)
    def _():
        for dma in page_copies(0, 0):
            dma.start()
        for dma in page_copies(jnp.minimum(1, last_step), 1):
            dma.start()

    @pl.when(s_id == 0)
    def _():
        m_s[...] = jnp.full_like(m_s, NEG_INF)
        l_s[...] = jnp.zeros_like(l_s)
        acc_s[...] = jnp.zeros_like(acc_s)

    for dma in page_copies(step, slot):
        dma.wait()

    def scores(c, bi):
        first = bi * pp + c * cp
        ck = ck_buf[slot, first:first + cp].reshape(cp * PAGE, ck_buf.shape[-1]).astype(BF16)
        kp_t = jnp.concatenate([kp_buf[slot, first + i] for i in range(cp)], axis=1).astype(BF16)
        s = (lax.dot_general(ql_ref[bi], ck, nt, preferred_element_type=F32)
             + jnp.dot(qp_ref[bi], kp_t, preferred_element_type=F32))
        return s, ck

    items = [(c, bi) for c in range(pp // cp) for bi in range(nb)]
    state = [(m_s[bi], l_s[bi], acc_s[bi]) for bi in range(nb)]
    ahead = {i: scores(*items[i]) for i in range(min(nb, len(items)))}
    for i, (c, bi) in enumerate(items):
        if i + nb < len(items):
            ahead[i + nb] = scores(*items[i + nb])
        s, ck = ahead.pop(i)
        state[bi] = _softmax_step(s, ck, *state[bi])
    for bi in range(nb):
        m_s[bi], l_s[bi], acc_s[bi] = state[bi]

    for dma in page_copies(jnp.minimum(step + 2, last_step), lax.rem(step + 2, PAGED_SLOTS)):
        dma.start()

    @pl.when(s_id == n_steps - 1)
    def _():
        for bi in range(nb):
            cn = cn_ref[bi]
            s_new = (lax.dot_general(ql_ref[bi], cn, nt, preferred_element_type=F32)
                     + lax.dot_general(qp_ref[bi], kn_ref[bi], nt, preferred_element_type=F32))
            tok = lax.broadcasted_iota(jnp.int32, s_new.shape, 0) % t_new
            key = lax.broadcasted_iota(jnp.int32, s_new.shape, 1)
            s_new = jnp.where(key <= tok, s_new, NEG_INF)
            _, l, acc = _softmax_step(s_new, cn, m_s[bi], l_s[bi], acc_s[bi])
            o_ref[bi] = (acc / l).astype(o_ref.dtype)

    @pl.when(step == last_step)
    def _():
        for ahead_by in (1, 2):
            for dma in page_copies(last_step, lax.rem(step + ahead_by, PAGED_SLOTS)):
                dma.wait()


def _paged_attention(q_lat, q_pe, c_new, k_new, cache_ckv, cache_kpe_t, page_table, *, nb, pp):
    bsz, rows, c_dim = q_lat.shape
    r_dim = q_pe.shape[-1]
    n_pages = page_table.shape[1]
    t_new = rows // N_HEADS
    kpad = c_new.shape[1]
    qmap = lambda b, s, pt: (b, 0, 0)
    any_spec = pl.BlockSpec(memory_space=pl.ANY)
    in_specs = [pl.BlockSpec((nb, rows, c_dim), qmap), pl.BlockSpec((nb, rows, r_dim), qmap),
                pl.BlockSpec((nb, kpad, c_dim), qmap), pl.BlockSpec((nb, kpad, r_dim), qmap),
                any_spec, any_spec]
    grid_spec = pltpu.PrefetchScalarGridSpec(
        num_scalar_prefetch=1, grid=(bsz // nb, n_pages // pp), in_specs=in_specs,
        out_specs=pl.BlockSpec((nb, rows, c_dim), qmap),
        scratch_shapes=[pltpu.VMEM((PAGED_SLOTS, nb * pp, PAGE, c_dim), F32),
                        pltpu.VMEM((PAGED_SLOTS, nb * pp, r_dim, PAGE), F32),
                        pltpu.SemaphoreType.DMA((2, PAGED_SLOTS)),
                        pltpu.VMEM((nb, rows, 1), F32), pltpu.VMEM((nb, rows, 1), F32),
                        pltpu.VMEM((nb, rows, c_dim), F32)])
    kern = functools.partial(_paged_kernel, nb=nb, pp=pp, cp=min(PAGED_KEY_CHUNK_PAGES, pp), n_pages=n_pages,
                             t_new=t_new)
    return pl.pallas_call(kern, grid_spec=grid_spec,
                          out_shape=jax.ShapeDtypeStruct((bsz, rows, c_dim), BF16),
                          compiler_params=_cparams("arbitrary", "arbitrary"), name="paged")(
        page_table.reshape(-1), q_lat, q_pe, c_new, k_new, cache_ckv, cache_kpe_t)


def _top2_kernel(lg_ref, idx_ref, gate_ref):
    lg = lg_ref[...]
    col = lax.broadcasted_iota(jnp.int32, lg.shape, 1).astype(F32)
    lg = jnp.where(col < N_EXPERTS, lg, NEG_INF)
    m1 = jnp.max(lg, axis=1, keepdims=True)
    i1 = jnp.min(jnp.where(lg == m1, col, float(LANES)), axis=1, keepdims=True)
    lg2 = jnp.where(col == i1, NEG_INF, lg)
    m2 = jnp.max(lg2, axis=1, keepdims=True)
    i2 = jnp.min(jnp.where(lg2 == m2, col, float(LANES)), axis=1, keepdims=True)
    e2 = jnp.exp(m2 - m1)
    den = 1.0 + e2
    idx_ref[...] = jnp.where(col == 0.0, i1, i2).astype(jnp.int32)
    gate_ref[...] = jnp.where(col == 0.0, 1.0 / den, e2 / den)


def _top2(logits):
    m = logits.shape[0]
    tm = _pick(m, (1024, 512, 256, 128, 64, 32, 16, 8))
    spec = pl.BlockSpec((tm, LANES), lambda i: (i, 0))
    return pl.pallas_call(_top2_kernel, grid=(m // tm,), in_specs=[spec], out_specs=[spec, spec],
                          out_shape=[jax.ShapeDtypeStruct((m, LANES), jnp.int32),
                                     jax.ShapeDtypeStruct((m, LANES), F32)],
                          compiler_params=_cparams("parallel"), name="top2")(logits)


ROW_COPY_UNROLL = 8


def _gather_into(src_hbm, dst_vmem, sem, n, src_row, dst_row):
    def copy(s_row, d_row):
        return pltpu.make_async_copy(src_hbm.at[pl.ds(s_row, 1), :], dst_vmem.at[pl.ds(d_row, 1), :], sem)

    assert n % ROW_COPY_UNROLL == 0

    def issue(g, carry):
        for u in range(ROW_COPY_UNROLL):
            r = g * ROW_COPY_UNROLL + u
            copy(src_row(r), dst_row(r)).start(priority=u % 2)
        return carry

    lax.fori_loop(0, n // ROW_COPY_UNROLL, issue, 0)

    def drain(g, carry):
        for _ in range(ROW_COPY_UNROLL):
            copy(0, 0).wait()
        return carry

    lax.fori_loop(0, n // ROW_COPY_UNROLL, drain, 0)


def _gather_rows_kernel(idx_ref, src_ref, o_ref, sem, *, tm):
    base = pl.program_id(0) * tm
    _gather_into(src_ref, o_ref, sem, tm, lambda r: idx_ref[base + r], lambda r: r)


def _gather_rows(src, idx, *, tm):
    n = idx.shape[0]
    d = src.shape[1]
    grid_spec = pltpu.PrefetchScalarGridSpec(
        num_scalar_prefetch=1, grid=(n // tm,), in_specs=[pl.BlockSpec(memory_space=pl.ANY)],
        out_specs=pl.BlockSpec((tm, d), lambda t, idx: (t, 0)), scratch_shapes=[pltpu.SemaphoreType.DMA(())])
    return pl.pallas_call(functools.partial(_gather_rows_kernel, tm=tm), grid_spec=grid_spec,
                          out_shape=jax.ShapeDtypeStruct((n, d), src.dtype),
                          compiler_params=_cparams("arbitrary"), name="gather_rows")(idx, src)


def _combine_kernel(pos_ref, h_ref, gt_ref, gain_ref, ys_ref, hs_ref, hn_ref, g_s, sem, *, tm):
    base = pl.program_id(0) * tm * TOP_K
    _gather_into(ys_ref, g_s, sem, tm * TOP_K, lambda a: pos_ref[base + a],
                 lambda a: lax.rem(a, TOP_K) * tm + lax.div(a, TOP_K))
    gt = gt_ref[...]
    y = gt[:, 0:1] * g_s[0:tm, :] + gt[:, 1:2] * g_s[tm:2 * tm, :]
    hs = h_ref[...] + y
    hs_ref[...] = hs
    yn = hs * lax.rsqrt(jnp.mean(hs * hs, axis=-1, keepdims=True) + EPS)
    hn_ref[...] = (yn * gain_ref[...]).astype(BF16)


def _combine(h, ys, pos, gates, gain, *, tm):
    m, d = h.shape
    row = lambda w: pl.BlockSpec((tm, w), lambda i, pos: (i, 0))
    grid_spec = pltpu.PrefetchScalarGridSpec(
        num_scalar_prefetch=1, grid=(m // tm,),
        in_specs=[row(d), row(LANES), pl.BlockSpec((1, d), lambda i, pos: (0, 0)), pl.BlockSpec(memory_space=pl.ANY)],
        out_specs=[row(d), row(d)],
        scratch_shapes=[pltpu.VMEM((TOP_K * tm, d), F32), pltpu.SemaphoreType.DMA(())])
    return pl.pallas_call(functools.partial(_combine_kernel, tm=tm), grid_spec=grid_spec,
                          out_shape=[jax.ShapeDtypeStruct((m, d), F32), jax.ShapeDtypeStruct((m, d), BF16)],
                          compiler_params=_cparams("arbitrary"), name="combine")(
        pos, h, gates, gain.reshape(1, d), ys)


MOE_TILE = 1024
MOE_SUB = 256
MOE_TF = 512
FFN_TILE = 1024
LRU_CHUNK = 512
FLASH_TILE = 1024
FLASH_ROW_CHUNK = 256
PAGED_KEY_CHUNK_PAGES = 8
PAGES_PER_STEP = 16
COMBINE_TILE = 256
PAGED_BATCH = 2
PAGED_SLOTS = 3


def _moe(xn_f32, router, w_gate, w_up, w_down):
    m, d = xn_f32.shape
    n_exp = router.shape[-1]
    router_p = jnp.pad(router, ((0, 0), (0, LANES - n_exp)))
    logits = _mm(xn_f32, router_p, tn=LANES)
    idx, gates = _top2(logits)
    flat_e = idx[:, :TOP_K].reshape(-1)
    n_assign = flat_e.shape[0]
    onehot = (flat_e[:, None] == jnp.arange(n_exp, dtype=jnp.int32)[None, :]).astype(jnp.int32)
    csum = jnp.cumsum(onehot, axis=0)
    rank = jnp.take_along_axis(csum, flat_e[:, None], axis=1)[:, 0] - 1
    counts = csum[-1]
    tiles_e = (counts + MOE_TILE - 1) // MOE_TILE
    tile_end = jnp.cumsum(tiles_e)
    tile_start = tile_end - tiles_e
    pos = (tile_start[flat_e] * MOE_TILE + rank).astype(jnp.int32)
    n_tiles = n_assign // MOE_TILE + n_exp
    tid = jnp.arange(n_tiles, dtype=jnp.int32)
    te = jnp.minimum(jnp.sum((tile_end[None, :] <= tid[:, None]).astype(jnp.int32), axis=1), n_exp - 1)
    active = tid < tile_end[-1]
    last_e = jnp.max(jnp.where(counts > 0, jnp.arange(n_exp, dtype=jnp.int32), 0))
    te = jnp.where(active, te, last_e).astype(jnp.int32)
    rows_left = counts[te] - (tid - tile_start[te]) * MOE_TILE
    nsub = jnp.where(active, (jnp.clip(rows_left, 0, MOE_TILE) + MOE_SUB - 1) // MOE_SUB, 0).astype(jnp.int32)

    p_rows = n_tiles * MOE_TILE
    tok = (jnp.arange(n_assign, dtype=jnp.int32) // TOP_K).astype(jnp.int32)
    slot_tok = (jnp.arange(p_rows, dtype=jnp.int32) % m).at[pos].set(tok)
    xs = _gather_rows(xn_f32, slot_tok, tm=MOE_TILE)
    ys = _ffn(xs, w_gate, w_up, w_down, te, nsub, tm=MOE_TILE, tf=MOE_TF, sub=MOE_SUB)
    return ys, pos, gates


def _rope_tables(pos):
    inv = 1.0 / (ROPE_THETA ** (jnp.arange(0, QK_ROPE, 2, dtype=F32) / QK_ROPE))
    ang = pos[:, None] * inv[None, :]
    return jnp.cos(ang), jnp.sin(ang)


def _rot_cols(w):
    half = w.shape[-1] // 2
    return jnp.concatenate([-w[..., half:], w[..., :half]], axis=-1)


def kernel(x_prompt, x_sample, p_prompt, p_sample, cache_ckv, cache_kpe, page_table, state_h, state_conv, norm_mix, norm_ffn, norm_ple, norm_final, lru_w_in, lru_conv_w, lru_conv_b, lru_w_a, lru_b_a, lru_w_i, lru_b_i, lru_lambda, lru_w_out, mla_wq_a, mla_q_norm, mla_wq_b, mla_wkv_a, mla_kv_norm, mla_wkv_b, mla_wo, ffn_w_gate, ffn_w_up, ffn_w_down, moe_router, moe_w_gate, moe_w_up, moe_w_down, ple_w_gate, ple_b_gate, ple_w_proj):
    bp, sp, d = x_prompt.shape
    bs, ts, _ = x_sample.shape
    n_p, n_s = bp * sp, bs * ts
    w_lru = lru_w_out.shape[1]
    past_len = page_table.shape[1] * PAGE

    h = jnp.concatenate([x_prompt.reshape(n_p, d), x_sample.reshape(n_s, d)], axis=0)
    ple_in = jnp.concatenate([p_prompt.reshape(p_prompt.shape[0], n_p, -1),
                              p_sample.reshape(p_sample.shape[0], n_s, -1)], axis=1)

    def ple(h, hn, i):
        return _mm(hn, ple_w_gate[i], bias=ple_b_gate[i], act="sigmoid", mul=(ple_in[i], ple_w_proj[i]), res=h)

    (xn,) = _resnorm(h, norm_mix[0])
    proj = _mm(xn, lru_w_in[0])
    proj_s = proj[n_p:].reshape(bs, ts, 2 * w_lru)
    lw = (lru_conv_w[0], lru_conv_b[0], lru_w_a[0], lru_b_a[0], lru_w_i[0], lru_b_i[0], lru_lambda[0])
    hy_p, hl_p = _lru(proj, *lw, bsz=bp, t_len=sp, chunk=min(LRU_CHUNK, sp))
    zpad = jnp.zeros((bs, 2 * SUBLANES - ts - (CONV_WIDTH - 1), w_lru), F32)
    xb_s = jnp.concatenate([zpad, state_conv[0], proj_s[..., :w_lru]], axis=1)
    yb_s = jnp.concatenate([jnp.zeros((bs, 2 * SUBLANES - ts, w_lru), F32), proj_s[..., w_lru:]], axis=1)
    h0_rows = jnp.zeros((bs, 2 * SUBLANES, w_lru), F32).at[:, 2 * SUBLANES - ts - 1].set(state_h[0])
    seg_rows = bs * 2 * SUBLANES
    hy_s, hf_s = _lru(jnp.concatenate([xb_s, yb_s], axis=-1).reshape(seg_rows, 2 * w_lru), *lw,
                      bsz=1, t_len=seg_rows, h0_rows=h0_rows.reshape(seg_rows, w_lru))
    hy_s = hy_s.reshape(bs, 2 * SUBLANES, w_lru)[:, 2 * SUBLANES - ts:]
    hf_s = hf_s.reshape(bs, 2 * SUBLANES, w_lru)
    hy = jnp.concatenate([hy_p, hy_s.reshape(n_s, w_lru)], axis=0)
    h = _mm(hy, lru_w_out[0], res=h)
    new_h_prompt = hl_p[:, SUBLANES - 1][None]
    new_h_sample = hf_s[:, -1][None]
    new_conv_prompt = jnp.stack([proj[(b + 1) * sp - (CONV_WIDTH - 1):(b + 1) * sp, :w_lru] for b in range(bp)])[None]
    new_conv_sample = xb_s[:, 2 * SUBLANES - (CONV_WIDTH - 1):][None]

    (xn,) = _resnorm(h, norm_ffn[0])
    tm_ffn = _pick(h.shape[0], (FFN_TILE, 512, 256))
    nt_ffn = h.shape[0] // tm_ffn
    f = _ffn(xn, ffn_w_gate, ffn_w_up, ffn_w_down, jnp.zeros((nt_ffn,), jnp.int32),
             jnp.full((nt_ffn,), tm_ffn // MOE_SUB, jnp.int32), tm=tm_ffn, tf=MOE_TF, sub=MOE_SUB)
    h, hn = _resnorm(h, norm_ple[0], add=f, want_sum=True)
    h = ple(h, hn, 0)

    (xn,) = _resnorm(h, norm_mix[1])
    cq = _mm(xn, mla_wq_a[0])
    (cqn,) = _resnorm(cq, mla_q_norm[0])
    pos_all = jnp.concatenate([jnp.tile(jnp.arange(sp, dtype=F32), bp),
                               jnp.tile(jnp.arange(ts, dtype=F32) + past_len, bs)])
    cos, sin = _rope_tables(pos_all)
    zeros_r = jnp.zeros((n_p + n_s, LANES - QK_ROPE), F32)
    cos_k = jnp.concatenate([cos, cos, zeros_r], axis=1)
    sin_k = jnp.concatenate([sin, sin, zeros_r], axis=1)
    ones_n = jnp.ones((n_p + n_s, QK_NOPE), F32)
    cos_q = jnp.tile(jnp.concatenate([ones_n, cos_k], axis=1), (1, 2))
    sin_q = jnp.tile(jnp.concatenate([0.0 * ones_n, sin_k], axis=1), (1, 2))
    wq_b = mla_wq_b[0]
    cq_dim = wq_b.shape[0]
    wq_pe = wq_b[..., QK_NOPE:]
    zq = jnp.zeros((cq_dim, N_HEADS, LANES - QK_ROPE), F32)
    wq_main = jnp.concatenate([wq_b[..., :QK_NOPE], wq_pe, zq], axis=-1).reshape(cq_dim, N_HEADS * 2 * LANES)
    wq_rot = jnp.concatenate([jnp.zeros((cq_dim, N_HEADS, QK_NOPE), F32), _rot_cols(wq_pe), zq],
                             axis=-1).reshape(cq_dim, N_HEADS * 2 * LANES)
    q_cat = _mm(cqn, wq_main, w2=wq_rot, cos=cos_q, sin=sin_q, out_dtype=BF16, tn=4 * LANES)

    wkv_a = mla_wkv_a[0]
    ckv_raw = _mm(xn, wkv_a[:, :KV_LORA])
    ckv_f32, ckv_bf = _resnorm(ckv_raw, mla_kv_norm[0], want_f32=True)
    zk = jnp.zeros((d, LANES - QK_ROPE), F32)
    wk_pe = wkv_a[:, KV_LORA:]
    kpe_pad = _mm(xn, jnp.concatenate([wk_pe, zk], axis=1),
                  w2=jnp.concatenate([_rot_cols(wk_pe), zk], axis=1), cos=cos_k, sin=sin_k, tn=LANES)
    kpe_bf = kpe_pad.astype(BF16)
    new_ckv_prompt = ckv_f32[:n_p].reshape(1, bp, sp, KV_LORA)
    new_ckv_sample = ckv_f32[n_p:].reshape(1, bs, ts, KV_LORA)
    new_kpe_prompt = kpe_pad[:n_p, :QK_ROPE].reshape(1, bp, sp, QK_ROPE)
    new_kpe_sample = kpe_pad[n_p:, :QK_ROPE].reshape(1, bs, ts, QK_ROPE)

    wkv_b = mla_wkv_b[0]
    wkv_up = jnp.concatenate([wkv_b[..., :QK_NOPE].reshape(KV_LORA, -1), wkv_b[..., QK_NOPE:].reshape(KV_LORA, -1)],
                             axis=1)
    kv_up = _mm(ckv_bf, wkv_up, out_dtype=BF16, m=n_p)
    t_att = _pick(sp, (FLASH_TILE, 512, 256, 128))
    o_p = _flash(q_cat, kv_up, kpe_bf, bsz=bp, s_len=sp, t=t_att)

    q_s = q_cat[n_p:].reshape(bs, ts, N_HEADS, 2 * LANES)
    qn_s = jnp.transpose(q_s[..., :QK_NOPE], (2, 0, 1, 3)).reshape(N_HEADS, n_s, QK_NOPE)
    w_uk_t = jnp.transpose(wkv_b[..., :QK_NOPE], (1, 2, 0))
    q_lat = _mm(qn_s, w_uk_t, out_dtype=BF16)
    q_lat = jnp.transpose(q_lat.reshape(N_HEADS, bs, ts, KV_LORA), (1, 0, 2, 3)).reshape(bs, N_HEADS * ts, KV_LORA)
    q_pe = jnp.transpose(q_s[..., QK_NOPE:QK_NOPE + QK_ROPE], (0, 2, 1, 3)).reshape(bs, N_HEADS * ts, QK_ROPE)
    kpad = 2 * SUBLANES
    c_new = jnp.pad(ckv_bf[n_p:].reshape(bs, ts, KV_LORA), ((0, 0), (0, kpad - ts), (0, 0)))
    k_new = jnp.pad(kpe_bf[n_p:, :QK_ROPE].reshape(bs, ts, QK_ROPE), ((0, 0), (0, kpad - ts), (0, 0)))
    o_lat = _paged_attention(q_lat, q_pe, c_new, k_new, cache_ckv[0], jnp.swapaxes(cache_kpe[0], 1, 2), page_table,
                             nb=PAGED_BATCH, pp=min(PAGES_PER_STEP, page_table.shape[1]))
    o_lat = jnp.transpose(o_lat.reshape(bs, N_HEADS, ts, KV_LORA), (1, 0, 2, 3)).reshape(N_HEADS, n_s, KV_LORA)
    w_uv = jnp.transpose(wkv_b[..., QK_NOPE:], (1, 0, 2))
    o_s = _mm(o_lat, w_uv, out_dtype=BF16, tn=V_HEAD)
    o_s = jnp.transpose(o_s, (1, 0, 2)).reshape(n_s, N_HEADS * V_HEAD)
    o_all = jnp.concatenate([o_p, o_s], axis=0)
    h = _mm(o_all, mla_wo[0], res=h)

    (xn_f32,) = _resnorm(h, norm_ffn[1], want_f32=True, want_bf16=False)
    ys, pos, gates = _moe(xn_f32, moe_router[0], moe_w_gate[0], moe_w_up[0], moe_w_down[0])
    h, hn = _combine(h, ys, pos, gates, norm_ple[1], tm=_pick(h.shape[0], (COMBINE_TILE, 128, 64, 32, 16, 8)))
    h = ple(h, hn, 1)

    (y_prompt,) = _resnorm(h, norm_final, want_f32=True, want_bf16=False, rows=(0, n_p))
    (y_sample,) = _resnorm(h, norm_final, want_f32=True, want_bf16=False, rows=(n_p, n_s))
    y_prompt = y_prompt.reshape(bp, sp, d)
    y_sample = y_sample.reshape(bs, ts, d)
    return (y_prompt, y_sample, new_ckv_prompt, new_kpe_prompt, new_ckv_sample, new_kpe_sample,
            new_h_prompt, new_conv_prompt, new_h_sample, new_conv_sample)
```

```python
import functools
import math

import jax
import jax.numpy as jnp
from jax import lax
from jax.experimental import pallas as pl
from jax.experimental.pallas import tpu as pltpu

F32 = jnp.float32
BF16 = jnp.bfloat16

N_HEADS = 16
QK_NOPE = 128
QK_ROPE = 64
V_HEAD = 128
KV_LORA = 512
LRU_HEADS = 8
LRU_C = 8.0
CONV_WIDTH = 4
N_EXPERTS = 8
TOP_K = 2
PAGE = 128
ROPE_THETA = 10000.0
EPS = 1e-6
MLA_SCALE = (QK_NOPE + QK_ROPE) ** -0.5
SCALE_LOG2E = MLA_SCALE * math.log2(math.e)

LANES = 128
SUBLANES = 8
VMEM_LIMIT_BYTES = 56 * 1024 * 1024

NEG_INF = float("-inf")


def _cparams(*sem):
    return pltpu.CompilerParams(dimension_semantics=sem, vmem_limit_bytes=VMEM_LIMIT_BYTES)


def _pick(n, prefs):
    for p in prefs:
        if n % p == 0:
            return p
    return n


def _resnorm_kernel(*refs, has_add, want_sum, want_f32, want_bf16):
    it = iter(refs)
    h_ref = next(it)
    f_ref = next(it) if has_add else None
    gain_ref = next(it)
    hs = h_ref[...]
    if has_add:
        hs = hs + f_ref[...]
    if want_sum:
        next(it)[...] = hs
    y = hs * lax.rsqrt(jnp.mean(hs * hs, axis=-1, keepdims=True) + EPS)
    y = y * gain_ref[...]
    if want_f32:
        next(it)[...] = y
    if want_bf16:
        next(it)[...] = y.astype(BF16)


def _resnorm(h, gain, *, add=None, want_sum=False, want_f32=False, want_bf16=True, tm=None, rows=None):
    d = h.shape[1]
    r0, m = rows if rows is not None else (0, h.shape[0])
    tm = tm or _pick(math.gcd(m, r0) if r0 else m, (512, 256, 128, 64, 32, 16, 8))
    b0 = r0 // tm
    row = pl.BlockSpec((tm, d), lambda i: (i, 0))
    row_in = pl.BlockSpec((tm, d), lambda i: (i + b0, 0))
    ins, specs = [h], [row_in]
    if add is not None:
        ins.append(add)
        specs.append(row_in)
    ins.append(gain.reshape(1, d))
    specs.append(pl.BlockSpec((1, d), lambda i: (0, 0)))
    outs, ospecs = [], []
    if want_sum:
        outs.append(jax.ShapeDtypeStruct((m, d), F32))
        ospecs.append(row)
    if want_f32:
        outs.append(jax.ShapeDtypeStruct((m, d), F32))
        ospecs.append(row)
    if want_bf16:
        outs.append(jax.ShapeDtypeStruct((m, d), BF16))
        ospecs.append(row)
    kern = functools.partial(_resnorm_kernel, has_add=add is not None,
                             want_sum=want_sum, want_f32=want_f32, want_bf16=want_bf16)
    return pl.pallas_call(kern, grid=(m // tm,), in_specs=specs, out_specs=ospecs, out_shape=outs,
                          compiler_params=_cparams("parallel"), name="resnorm")(*ins)


def _mm_kernel(*refs, nk, act, rope, has_bias, has_mul, mul_is_product, has_res):
    it = iter(refs)
    x_ref = next(it)
    w_ref = next(it)
    w2_ref = next(it) if rope else None
    c_ref = next(it) if rope else None
    s_ref = next(it) if rope else None
    b_ref = next(it) if has_bias else None
    m_ref = next(it) if has_mul else None
    m_w_ref = next(it) if mul_is_product else None
    r_ref = next(it) if has_res else None
    o_ref = next(it)
    acc_ref = next(it) if nk > 1 else None

    x = x_ref[...].astype(BF16)
    p = jnp.dot(x, w_ref[...].astype(BF16), preferred_element_type=F32)
    if rope:
        p2 = jnp.dot(x, w2_ref[...].astype(BF16), preferred_element_type=F32)
        p = p * c_ref[...] + p2 * s_ref[...]

    def finish(acc):
        if has_bias:
            acc = acc + b_ref[...]
        if act == "sigmoid":
            acc = jax.nn.sigmoid(acc)
        if mul_is_product:
            acc = acc * jnp.dot(m_ref[...].astype(BF16), m_w_ref[...].astype(BF16), preferred_element_type=F32)
        elif has_mul:
            acc = m_ref[...] * acc
        if has_res:
            acc = r_ref[...] + acc
        o_ref[...] = acc.astype(o_ref.dtype)

    if nk == 1:
        finish(p)
    else:
        k = pl.program_id(3)

        @pl.when(k == 0)
        def _():
            acc_ref[...] = p

        @pl.when(k > 0)
        def _():
            acc_ref[...] += p

        @pl.when(k == nk - 1)
        def _():
            finish(acc_ref[...])


def _mm(x, w, *, w2=None, cos=None, sin=None, bias=None, act=None, mul=None, res=None,
        out_dtype=F32, tm=None, tn=None, tk=None, m=None):
    squeeze = x.ndim == 2
    mul_is_product = isinstance(mul, tuple)
    if squeeze:
        x, w = x[None], w[None]
        w2 = None if w2 is None else w2[None]
        cos = None if cos is None else cos[None]
        sin = None if sin is None else sin[None]
        bias = None if bias is None else bias.reshape(1, 1, -1)
        if mul_is_product:
            mul = (mul[0][None], mul[1][None])
        elif mul is not None:
            mul = mul[None]
        res = None if res is None else res[None]
    g, m_all, kd = x.shape
    m = m or m_all
    n = w.shape[-1]
    tm = tm or _pick(m, (1024, 512, 256, 128, 64, 32, 16, 8))
    tn = tn or _pick(n, (1024, 512, 256, 128) if mul is None else (512, 256, 128))
    tk = tk or (kd if kd <= 2048 else _pick(kd, (1024, 512)))
    nk = kd // tk
    rope = w2 is not None
    xs = pl.BlockSpec((None, tm, tk), lambda gi, i, j, k: (gi, i, k))
    ws = pl.BlockSpec((None, tk, tn), lambda gi, i, j, k: (gi, k, j))
    os_ = pl.BlockSpec((None, tm, tn), lambda gi, i, j, k: (gi, i, j))
    ins, specs = [x, w], [xs, ws]
    if rope:
        ncb = cos.shape[-1] // tn
        tab = pl.BlockSpec((None, tm, tn), lambda gi, i, j, k: (gi, i, j % ncb))
        ins += [w2, cos, sin]
        specs += [ws, tab, tab]
    if bias is not None:
        ins.append(bias)
        specs.append(pl.BlockSpec((None, 1, tn), lambda gi, i, j, k: (gi, 0, j)))
    if mul_is_product:
        assert nk == 1
        k2 = mul[0].shape[-1]
        ins += [mul[0], mul[1]]
        specs += [pl.BlockSpec((None, tm, k2), lambda gi, i, j, k: (gi, i, 0)),
                  pl.BlockSpec((None, k2, tn), lambda gi, i, j, k: (gi, 0, j))]
    elif mul is not None:
        ins.append(mul)
        specs.append(os_)
    if res is not None:
        ins.append(res)
        specs.append(os_)
    kern = functools.partial(_mm_kernel, nk=nk, act=act, rope=rope, has_bias=bias is not None,
                             has_mul=mul is not None, mul_is_product=mul_is_product, has_res=res is not None)
    scratch = [pltpu.VMEM((tm, tn), F32)] if nk > 1 else []
    out = pl.pallas_call(kern, grid=(g, m // tm, n // tn, nk), in_specs=specs, out_specs=os_,
                         out_shape=jax.ShapeDtypeStruct((g, m, n), out_dtype), scratch_shapes=scratch,
                         compiler_params=_cparams("parallel", "parallel", "parallel", "arbitrary"),
                         name="mm")(*ins)
    return out[0] if squeeze else out


def _ffn_kernel(te_ref, ns_ref, x_ref, wg_ref, wu_ref, wd_ref, o_ref, wg_s, wu_s, wd_s, *, sub):
    t = pl.program_id(0)
    j = pl.program_id(1)
    ns = ns_ref[t]

    @pl.when(j == 0)
    def _():
        o_ref[...] = jnp.zeros_like(o_ref)

    def block(r, rows, wg, wu, wd):
        xs = x_ref[pl.ds(r, rows), :].astype(BF16)
        gate = jnp.dot(xs, wg, preferred_element_type=F32)
        up = jnp.dot(xs, wu, preferred_element_type=F32)
        mid = (jax.nn.silu(gate) * up).astype(BF16)
        o_ref[pl.ds(r, rows), :] += jnp.dot(mid, wd, preferred_element_type=F32)

    def first_block(rows):
        wg = wg_ref[...].astype(BF16)
        wu = wu_ref[...].astype(BF16)
        wd = wd_ref[...].astype(BF16)
        wg_s[...] = wg
        wu_s[...] = wu
        wd_s[...] = wd
        block(0, rows, wg, wu, wd)

    @pl.when(ns == 1)
    def _():
        first_block(sub)

    @pl.when(ns >= 2)
    def _():
        first_block(2 * sub)

    @pl.when(ns > 2)
    def _():
        rest = ns - 2
        n_pairs = lax.shift_right_logical(rest, 1)

        def pair(p, carry):
            block(pl.multiple_of((p + 1) * (2 * sub), sub), 2 * sub, wg_s[...], wu_s[...], wd_s[...])
            return carry

        lax.fori_loop(0, n_pairs, pair, 0)

        @pl.when((rest & 1) == 1)
        def _():
            block(pl.multiple_of((n_pairs + 1) * (2 * sub), sub), sub, wg_s[...], wu_s[...], wd_s[...])


def _ffn(x, w_gate, w_up, w_down, tile_expert, tile_nsub, *, tm, tf, sub):
    p_rows, d = x.shape
    f = w_gate.shape[-1]
    nt, nj = p_rows // tm, f // tf

    def jj(t, j, ns):
        return jnp.where(ns[t] > 0, j, nj - 1)

    once = pl.Buffered(1)
    grid_spec = pltpu.PrefetchScalarGridSpec(
        num_scalar_prefetch=2, grid=(nt, nj),
        in_specs=[
            pl.BlockSpec((tm, d), lambda t, j, te, ns: (t, 0), pipeline_mode=once),
            pl.BlockSpec((None, d, tf), lambda t, j, te, ns: (te[t], 0, jj(t, j, ns))),
            pl.BlockSpec((None, d, tf), lambda t, j, te, ns: (te[t], 0, jj(t, j, ns))),
            pl.BlockSpec((None, tf, d), lambda t, j, te, ns: (te[t], jj(t, j, ns), 0)),
        ],
        out_specs=pl.BlockSpec((tm, d), lambda t, j, te, ns: (t, 0), pipeline_mode=once),
        scratch_shapes=[pltpu.VMEM((d, tf), BF16), pltpu.VMEM((d, tf), BF16), pltpu.VMEM((tf, d), BF16)])
    return pl.pallas_call(functools.partial(_ffn_kernel, sub=sub), grid_spec=grid_spec,
                          out_shape=jax.ShapeDtypeStruct((p_rows, d), F32),
                          compiler_params=_cparams("parallel", "arbitrary"),
                          name="ffn")(tile_expert, tile_nsub, x, w_gate, w_up, w_down)


def _softplus(x):
    return jnp.maximum(x, 0.0) + jnp.log1p(jnp.exp(-jnp.abs(x)))


def _lru_kernel(*refs, seg, carry, pre, rows):
    it = iter(refs)
    xb_ref = next(it)
    yb_ref = next(it)
    cw_ref = next(it)
    cb_ref = next(it)
    wa_ref = next(it)
    ba_ref = next(it)
    wi_ref = next(it)
    bi_ref = next(it)
    lam_ref = next(it)
    h0_ref = next(it) if pre else None
    hy_ref = next(it)
    h_ref = next(it)
    xpad_s = next(it)
    hc_s = next(it)

    c = pl.program_id(2)
    if carry:
        @pl.when(c == 0)
        def _():
            xpad_s[0:SUBLANES, :] = jnp.zeros((SUBLANES, xpad_s.shape[1]), F32)
            hc_s[...] = jnp.zeros_like(hc_s)
    else:
        xpad_s[0:SUBLANES, :] = jnp.zeros((SUBLANES, xpad_s.shape[1]), F32)

    x = xb_ref[...]
    xpad_s[pl.ds(SUBLANES, rows), :] = x
    cw = cw_ref[...]
    xc = cb_ref[...]
    for k in range(CONV_WIDTH):
        shift = CONV_WIDTH - 1 - k
        xk = x if shift == 0 else xpad_s[pl.ds(SUBLANES - shift, rows), :]
        xc = xc + xk * cw[k:k + 1, :]
    if carry:
        xpad_s[0:SUBLANES, :] = x[rows - SUBLANES:, :]

    xcb = xc.astype(BF16)
    ga = jnp.dot(xcb, wa_ref[...].astype(BF16), preferred_element_type=F32) + ba_ref[...]
    gi = jnp.dot(xcb, wi_ref[...].astype(BF16), preferred_element_type=F32) + bi_ref[...]
    r = jax.nn.sigmoid(ga)
    ig = jax.nn.sigmoid(gi)
    log_a = (-LRU_C * r) * _softplus(-lam_ref[...])
    a = jnp.exp(log_a)
    mult = jnp.sqrt(-jnp.tanh(log_a) * (a * a + 1.0))
    bx = mult * ig * xc

    row = lax.broadcasted_iota(jnp.int32, a.shape, 0)
    rmod = row & (seg - 1)
    if pre:
        valid = rmod >= seg // 2
        a = jnp.where(valid, a, 0.0)
        bx = jnp.where(valid, bx, h0_ref[...])

    av, bv = a, bx
    dist = 1
    while dist < seg:
        keep = rmod >= dist
        a_sh = jnp.where(keep, pltpu.roll(av, dist, 0), 1.0)
        b_sh = jnp.where(keep, pltpu.roll(bv, dist, 0), 0.0)
        bv = av * b_sh + bv
        av = av * a_sh
        dist *= 2
    if carry:
        h = av * hc_s[SUBLANES - 1:SUBLANES, :] + bv
        hc_s[...] = h[rows - SUBLANES:, :]
    else:
        h = bv
    hy_ref[...] = (h * jax.nn.gelu(yb_ref[...], approximate=True)).astype(BF16)
    if carry:
        h_ref[...] = h[rows - SUBLANES:, :]
    else:
        h_ref[...] = h


def _lru(proj, conv_w, conv_b, w_a, b_a, w_i, b_i, lam, *, bsz, t_len, h0_rows=None, chunk=None):
    w = proj.shape[1] // 2
    hb = w // LRU_HEADS
    pre = h0_rows is not None
    carry = not pre
    rows = chunk if carry else t_len
    seg = chunk if carry else 2 * SUBLANES
    nc = t_len // rows
    blk = lambda off: pl.BlockSpec((rows, hb), lambda b, h, c: (b * nc + c, h + off))
    vec = pl.BlockSpec((1, hb), lambda b, h, c: (0, h))
    mat = pl.BlockSpec((None, hb, hb), lambda b, h, c: (h, 0, 0))
    hvec = pl.BlockSpec((None, 1, hb), lambda b, h, c: (h, 0, 0))
    ins = [proj, proj, conv_w, conv_b.reshape(1, w), w_a, b_a.reshape(LRU_HEADS, 1, hb),
           w_i, b_i.reshape(LRU_HEADS, 1, hb), lam.reshape(1, w)]
    specs = [blk(0), blk(LRU_HEADS), pl.BlockSpec((CONV_WIDTH, hb), lambda b, h, c: (0, h)), vec,
             mat, hvec, mat, hvec, vec]
    if pre:
        ins.append(h0_rows)
        specs.append(blk(0))
    h_rows = SUBLANES if carry else t_len
    outs = [jax.ShapeDtypeStruct((bsz * t_len, w), BF16), jax.ShapeDtypeStruct((bsz, h_rows, w), F32)]
    ospecs = [blk(0), pl.BlockSpec((None, h_rows, hb), lambda b, h, c: (b, 0, h))]
    kern = functools.partial(_lru_kernel, seg=seg, carry=carry, pre=pre, rows=rows)
    return pl.pallas_call(kern, grid=(bsz, LRU_HEADS, nc), in_specs=specs, out_specs=ospecs, out_shape=outs,
                          scratch_shapes=[pltpu.VMEM((rows + SUBLANES, hb), F32), pltpu.VMEM((SUBLANES, hb), F32)],
                          compiler_params=_cparams("parallel", "parallel", "arbitrary"), name="lru")(*ins)


def _softmax_step(s, vals, m, l, acc):
    m_new = jnp.maximum(m, jnp.max(s, axis=1, keepdims=True))
    alpha = jnp.exp2((m - m_new) * SCALE_LOG2E)
    p = jnp.exp2((s - m_new) * SCALE_LOG2E)
    l = alpha * l + jnp.sum(p, axis=1, keepdims=True)
    acc = alpha * acc + jnp.dot(p.astype(BF16), vals, preferred_element_type=F32)
    return m_new, l, acc


def _flash_kernel(qi_ref, ki_ref, q_ref, kn_ref, kp_ref, v_ref, o_ref, m_s, l_s, acc_s, *, t, rc):
    p_id = pl.program_id(2)
    qi = qi_ref[p_id]
    ki = ki_ref[p_id]

    @pl.when(ki == 0)
    def _():
        m_s[...] = jnp.full_like(m_s, NEG_INF)
        l_s[...] = jnp.zeros_like(l_s)
        acc_s[...] = jnp.zeros_like(acc_s)

    def block(diagonal):
        k = jnp.concatenate([kn_ref[...], kp_ref[...]], axis=1)
        n_chunks = t // rc

        def keys_of(c):
            return (c + 1) * rc if diagonal else t

        def scores(c):
            s = lax.dot_general(q_ref[pl.ds(c * rc, rc), :], k[:keys_of(c)], (((1,), (1,)), ((), ())),
                                preferred_element_type=F32)
            if diagonal:
                q_pos = c * rc + lax.broadcasted_iota(jnp.int32, s.shape, 0)
                k_pos = lax.broadcasted_iota(jnp.int32, s.shape, 1)
                s = jnp.where(k_pos <= q_pos, s, NEG_INF)
            return s

        s_next = scores(0)
        for c in range(n_chunks):
            rows = pl.ds(c * rc, rc)
            s = s_next
            if c + 1 < n_chunks:
                s_next = scores(c + 1)
            m, l, acc = _softmax_step(s, v_ref[0:keys_of(c), :], m_s[rows, :], l_s[rows, :], acc_s[rows, :])
            m_s[rows, :] = m
            l_s[rows, :] = l
            acc_s[rows, :] = acc

    @pl.when(ki == qi)
    def _():
        block(True)
        o_ref[...] = (acc_s[...] / l_s[...]).astype(o_ref.dtype)

    @pl.when(ki != qi)
    def _():
        block(False)


def _flash(q_cat, kv_up, kpe, *, bsz, s_len, t):
    tq = tk = t
    nq = nkb = s_len // t
    pairs = [(qb, kb) for qb in range(nq) for kb in range(qb + 1)]
    qidx = jnp.asarray([p[0] for p in pairs], jnp.int32)
    kidx = jnp.asarray([p[1] for p in pairs], jnp.int32)
    hd = QK_NOPE
    grid_spec = pltpu.PrefetchScalarGridSpec(
        num_scalar_prefetch=2, grid=(bsz, N_HEADS, len(pairs)),
        in_specs=[
            pl.BlockSpec((tq, 2 * hd), lambda b, h, p, qi, ki: (b * nq + qi[p], h)),
            pl.BlockSpec((tk, hd), lambda b, h, p, qi, ki: (b * nkb + ki[p], h)),
            pl.BlockSpec((tk, hd), lambda b, h, p, qi, ki: (b * nkb + ki[p], 0)),
            pl.BlockSpec((tk, hd), lambda b, h, p, qi, ki: (b * nkb + ki[p], N_HEADS + h)),
        ],
        out_specs=pl.BlockSpec((tq, hd), lambda b, h, p, qi, ki: (b * nq + qi[p], h)),
        scratch_shapes=[pltpu.VMEM((tq, 1), F32), pltpu.VMEM((tq, 1), F32), pltpu.VMEM((tq, hd), F32)])
    return pl.pallas_call(functools.partial(_flash_kernel, t=t, rc=min(FLASH_ROW_CHUNK, t)), grid_spec=grid_spec,
                          out_shape=jax.ShapeDtypeStruct((bsz * s_len, N_HEADS * hd), BF16),
                          compiler_params=_cparams("parallel", "parallel", "arbitrary"),
                          name="flash")(qidx, kidx, q_cat, kv_up, kpe, kv_up)


def _paged_kernel(pt_ref, ql_ref, qp_ref, cn_ref, kn_ref, ckv_hbm, kpe_hbm, o_ref,
                  ck_buf, kp_buf, sems, m_s, l_s, acc_s, *, nb, pp, cp, n_pages, t_new):
    n_steps = n_pages // pp
    s_id = pl.program_id(1)
    step = pl.program_id(0) * n_steps + s_id
    last_step = pl.num_programs(0) * n_steps - 1
    slot = lax.rem(step, PAGED_SLOTS)
    nt = (((1,), (1,)), ((), ()))

    def page_copies(for_step, for_slot):
        first_b = lax.div(for_step, n_steps) * nb
        first_page = lax.rem(for_step, n_steps) * pp
        copies = []
        for bi in range(nb):
            for i in range(pp):
                pg = pt_ref[(first_b + bi) * n_pages + first_page + i]
                dst = bi * pp + i
                copies.append(pltpu.make_async_copy(ckv_hbm.at[pg], ck_buf.at[for_slot, dst], sems.at[0, for_slot]))
                copies.append(pltpu.make_async_copy(kpe_hbm.at[pg], kp_buf.at[for_slot, dst], sems.at[1, for_slot]))
        return copies

    @pl.when(step == 0)
    def _():
        for dma in page_copies(0, 0):
            dma.start()
        for dma in page_copies(jnp.minimum(1, last_step), 1):
            dma.start()

    @pl.when(s_id == 0)
    def _():
        m_s[...] = jnp.full_like(m_s, NEG_INF)
        l_s[...] = jnp.zeros_like(l_s)
        acc_s[...] = jnp.zeros_like(acc_s)

    for dma in page_copies(step, slot):
        dma.wait()

    def scores(c, bi):
        first = bi * pp + c * cp
        ck = ck_buf[slot, first:first + cp].reshape(cp * PAGE, ck_buf.shape[-1]).astype(BF16)
        kp_t = jnp.concatenate([kp_buf[slot, first + i] for i in range(cp)], axis=1).astype(BF16)
        s = (lax.dot_general(ql_ref[bi], ck, nt, preferred_element_type=F32)
             + jnp.dot(qp_ref[bi], kp_t, preferred_element_type=F32))
        return s, ck

    items = [(c, bi) for c in range(pp // cp) for bi in range(nb)]
    state = [(m_s[bi], l_s[bi], acc_s[bi]) for bi in range(nb)]
    ahead = {i: scores(*items[i]) for i in range(min(nb, len(items)))}
    for i, (c, bi) in enumerate(items):
        if i + nb < len(items):
            ahead[i + nb] = scores(*items[i + nb])
        s, ck = ahead.pop(i)
        state[bi] = _softmax_step(s, ck, *state[bi])
    for bi in range(nb):
        m_s[bi], l_s[bi], acc_s[bi] = state[bi]

    for dma in page_copies(jnp.minimum(step + 2, last_step), lax.rem(step + 2, PAGED_SLOTS)):
        dma.start()

    @pl.when(s_id == n_steps - 1)
    def _():
        for bi in range(nb):
            cn = cn_ref[bi]
            s_new = (lax.dot_general(ql_ref[bi], cn, nt, preferred_element_type=F32)
                     + lax.dot_general(qp_ref[bi], kn_ref[bi], nt, preferred_element_type=F32))
            tok = lax.broadcasted_iota(jnp.int32, s_new.shape, 0) % t_new
            key = lax.broadcasted_iota(jnp.int32, s_new.shape, 1)
            s_new = jnp.where(key <= tok, s_new, NEG_INF)
            _, l, acc = _softmax_step(s_new, cn, m_s[bi], l_s[bi], acc_s[bi])
            o_ref[bi] = (acc / l).astype(o_ref.dtype)

    @pl.when(step == last_step)
    def _():
        for ahead_by in (1, 2):
            for dma in page_copies(last_step, lax.rem(step + ahead_by, PAGED_SLOTS)):
                dma.wait()


def _paged_attention(q_lat, q_pe, c_new, k_new, cache_ckv, cache_kpe_t, page_table, *, nb, pp):
    bsz, rows, c_dim = q_lat.shape
    r_dim = q_pe.shape[-1]
    n_pages = page_table.shape[1]
    t_new = rows // N_HEADS
    kpad = c_new.shape[1]
    qmap = lambda b, s, pt: (b, 0, 0)
    any_spec = pl.BlockSpec(memory_space=pl.ANY)
    in_specs = [pl.BlockSpec((nb, rows, c_dim), qmap), pl.BlockSpec((nb, rows, r_dim), qmap),
                pl.BlockSpec((nb, kpad, c_dim), qmap), pl.BlockSpec((nb, kpad, r_dim), qmap),
                any_spec, any_spec]
    grid_spec = pltpu.PrefetchScalarGridSpec(
        num_scalar_prefetch=1, grid=(bsz // nb, n_pages // pp), in_specs=in_specs,
        out_specs=pl.BlockSpec((nb, rows, c_dim), qmap),
        scratch_shapes=[pltpu.VMEM((PAGED_SLOTS, nb * pp, PAGE, c_dim), F32),
                        pltpu.VMEM((PAGED_SLOTS, nb * pp, r_dim, PAGE), F32),
                        pltpu.SemaphoreType.DMA((2, PAGED_SLOTS)),
                        pltpu.VMEM((nb, rows, 1), F32), pltpu.VMEM((nb, rows, 1), F32),
                        pltpu.VMEM((nb, rows, c_dim), F32)])
    kern = functools.partial(_paged_kernel, nb=nb, pp=pp, cp=min(PAGED_KEY_CHUNK_PAGES, pp), n_pages=n_pages,
                             t_new=t_new)
    return pl.pallas_call(kern, grid_spec=grid_spec,
                          out_shape=jax.ShapeDtypeStruct((bsz, rows, c_dim), BF16),
                          compiler_params=_cparams("arbitrary", "arbitrary"), name="paged")(
        page_table.reshape(-1), q_lat, q_pe, c_new, k_new, cache_ckv, cache_kpe_t)


def _top2_kernel(lg_ref, idx_ref, gate_ref):
    lg = lg_ref[...]
    col = lax.broadcasted_iota(jnp.int32, lg.shape, 1).astype(F32)
    lg = jnp.where(col < N_EXPERTS, lg, NEG_INF)
    m1 = jnp.max(lg, axis=1, keepdims=True)
    i1 = jnp.min(jnp.where(lg == m1, col, float(LANES)), axis=1, keepdims=True)
    lg2 = jnp.where(col == i1, NEG_INF, lg)
    m2 = jnp.max(lg2, axis=1, keepdims=True)
    i2 = jnp.min(jnp.where(lg2 == m2, col, float(LANES)), axis=1, keepdims=True)
    e2 = jnp.exp(m2 - m1)
    den = 1.0 + e2
    idx_ref[...] = jnp.where(col == 0.0, i1, i2).astype(jnp.int32)
    gate_ref[...] = jnp.where(col == 0.0, 1.0 / den, e2 / den)


def _top2(logits):
    m = logits.shape[0]
    tm = _pick(m, (1024, 512, 256, 128, 64, 32, 16, 8))
    spec = pl.BlockSpec((tm, LANES), lambda i: (i, 0))
    return pl.pallas_call(_top2_kernel, grid=(m // tm,), in_specs=[spec], out_specs=[spec, spec],
                          out_shape=[jax.ShapeDtypeStruct((m, LANES), jnp.int32),
                                     jax.ShapeDtypeStruct((m, LANES), F32)],
                          compiler_params=_cparams("parallel"), name="top2")(logits)


ROW_COPY_UNROLL = 8


def _gather_into(src_hbm, dst_vmem, sem, n, src_row, dst_row):
    def copy(s_row, d_row):
        return pltpu.make_async_copy(src_hbm.at[pl.ds(s_row, 1), :], dst_vmem.at[pl.ds(d_row, 1), :], sem)

    assert n % ROW_COPY_UNROLL == 0

    def issue(g, carry):
        for u in range(ROW_COPY_UNROLL):
            r = g * ROW_COPY_UNROLL + u
            copy(src_row(r), dst_row(r)).start(priority=u % 2)
        return carry

    lax.fori_loop(0, n // ROW_COPY_UNROLL, issue, 0)

    def drain(g, carry):
        for _ in range(ROW_COPY_UNROLL):
            copy(0, 0).wait()
        return carry

    lax.fori_loop(0, n // ROW_COPY_UNROLL, drain, 0)


def _gather_rows_kernel(idx_ref, src_ref, o_ref, sem, *, tm):
    base = pl.program_id(0) * tm
    _gather_into(src_ref, o_ref, sem, tm, lambda r: idx_ref[base + r], lambda r: r)


def _gather_rows(src, idx, *, tm):
    n = idx.shape[0]
    d = src.shape[1]
    grid_spec = pltpu.PrefetchScalarGridSpec(
        num_scalar_prefetch=1, grid=(n // tm,), in_specs=[pl.BlockSpec(memory_space=pl.ANY)],
        out_specs=pl.BlockSpec((tm, d), lambda t, idx: (t, 0)), scratch_shapes=[pltpu.SemaphoreType.DMA(())])
    return pl.pallas_call(functools.partial(_gather_rows_kernel, tm=tm), grid_spec=grid_spec,
                          out_shape=jax.ShapeDtypeStruct((n, d), src.dtype),
                          compiler_params=_cparams("arbitrary"), name="gather_rows")(idx, src)


def _combine_kernel(pos_ref, h_ref, gt_ref, gain_ref, ys_ref, hs_ref, hn_ref, g_s, sem, *, tm):
    base = pl.program_id(0) * tm * TOP_K
    _gather_into(ys_ref, g_s, sem, tm * TOP_K, lambda a: pos_ref[base + a],
                 lambda a: lax.rem(a, TOP_K) * tm + lax.div(a, TOP_K))
    gt = gt_ref[...]
    y = gt[:, 0:1] * g_s[0:tm, :] + gt[:, 1:2] * g_s[tm:2 * tm, :]
    hs = h_ref[...] + y
    hs_ref[...] = hs
    yn = hs * lax.rsqrt(jnp.mean(hs * hs, axis=-1, keepdims=True) + EPS)
    hn_ref[...] = (yn * gain_ref[...]).astype(BF16)


def _combine(h, ys, pos, gates, gain, *, tm):
    m, d = h.shape
    row = lambda w: pl.BlockSpec((tm, w), lambda i, pos: (i, 0))
    grid_spec = pltpu.PrefetchScalarGridSpec(
        num_scalar_prefetch=1, grid=(m // tm,),
        in_specs=[row(d), row(LANES), pl.BlockSpec((1, d), lambda i, pos: (0, 0)), pl.BlockSpec(memory_space=pl.ANY)],
        out_specs=[row(d), row(d)],
        scratch_shapes=[pltpu.VMEM((TOP_K * tm, d), F32), pltpu.SemaphoreType.DMA(())])
    return pl.pallas_call(functools.partial(_combine_kernel, tm=tm), grid_spec=grid_spec,
                          out_shape=[jax.ShapeDtypeStruct((m, d), F32), jax.ShapeDtypeStruct((m, d), BF16)],
                          compiler_params=_cparams("arbitrary"), name="combine")(
        pos, h, gates, gain.reshape(1, d), ys)


MOE_TILE = 1024
MOE_SUB = 256
MOE_TF = 512
FFN_TILE = 1024
LRU_CHUNK = 512
FLASH_TILE = 1024
FLASH_ROW_CHUNK = 256
PAGED_KEY_CHUNK_PAGES = 8
PAGES_PER_STEP = 8
COMBINE_TILE = 512
PAGED_BATCH = 4
PAGED_SLOTS = 3


def _moe(xn_f32, router, w_gate, w_up, w_down):
    m, d = xn_f32.shape
    n_exp = router.shape[-1]
    router_p = jnp.pad(router, ((0, 0), (0, LANES - n_exp)))
    logits = _mm(xn_f32, router_p, tn=LANES)
    idx, gates = _top2(logits)
    flat_e = idx[:, :TOP_K].reshape(-1)
    n_assign = flat_e.shape[0]
    onehot = (flat_e[:, None] == jnp.arange(n_exp, dtype=jnp.int32)[None, :]).astype(jnp.int32)
    csum = jnp.cumsum(onehot, axis=0)
    rank = jnp.take_along_axis(csum, flat_e[:, None], axis=1)[:, 0] - 1
    counts = csum[-1]
    tiles_e = (counts + MOE_TILE - 1) // MOE_TILE
    tile_end = jnp.cumsum(tiles_e)
    tile_start = tile_end - tiles_e
    pos = (tile_start[flat_e] * MOE_TILE + rank).astype(jnp.int32)
    n_tiles = n_assign // MOE_TILE + n_exp
    tid = jnp.arange(n_tiles, dtype=jnp.int32)
    te = jnp.minimum(jnp.sum((tile_end[None, :] <= tid[:, None]).astype(jnp.int32), axis=1), n_exp - 1)
    active = tid < tile_end[-1]
    last_e = jnp.max(jnp.where(counts > 0, jnp.arange(n_exp, dtype=jnp.int32), 0))
    te = jnp.where(active, te, last_e).astype(jnp.int32)
    rows_left = counts[te] - (tid - tile_start[te]) * MOE_TILE
    nsub = jnp.where(active, (jnp.clip(rows_left, 0, MOE_TILE) + MOE_SUB - 1) // MOE_SUB, 0).astype(jnp.int32)

    p_rows = n_tiles * MOE_TILE
    tok = (jnp.arange(n_assign, dtype=jnp.int32) // TOP_K).astype(jnp.int32)
    slot_tok = (jnp.arange(p_rows, dtype=jnp.int32) % m).at[pos].set(tok)
    xs = _gather_rows(xn_f32, slot_tok, tm=MOE_TILE)
    ys = _ffn(xs, w_gate, w_up, w_down, te, nsub, tm=MOE_TILE, tf=MOE_TF, sub=MOE_SUB)
    return ys, pos, gates


def _rope_tables(pos):
    inv = 1.0 / (ROPE_THETA ** (jnp.arange(0, QK_ROPE, 2, dtype=F32) / QK_ROPE))
    ang = pos[:, None] * inv[None, :]
    return jnp.cos(ang), jnp.sin(ang)


def _rot_cols(w):
    half = w.shape[-1] // 2
    return jnp.concatenate([-w[..., half:], w[..., :half]], axis=-1)


def kernel(x_prompt, x_sample, p_prompt, p_sample, cache_ckv, cache_kpe, page_table, state_h, state_conv, norm_mix, norm_ffn, norm_ple, norm_final, lru_w_in, lru_conv_w, lru_conv_b, lru_w_a, lru_b_a, lru_w_i, lru_b_i, lru_lambda, lru_w_out, mla_wq_a, mla_q_norm, mla_wq_b, mla_wkv_a, mla_kv_norm, mla_wkv_b, mla_wo, ffn_w_gate, ffn_w_up, ffn_w_down, moe_router, moe_w_gate, moe_w_up, moe_w_down, ple_w_gate, ple_b_gate, ple_w_proj):
    bp, sp, d = x_prompt.shape
    bs, ts, _ = x_sample.shape
    n_p, n_s = bp * sp, bs * ts
    w_lru = lru_w_out.shape[1]
    past_len = page_table.shape[1] * PAGE

    h = jnp.concatenate([x_prompt.reshape(n_p, d), x_sample.reshape(n_s, d)], axis=0)
    ple_in = jnp.concatenate([p_prompt.reshape(p_prompt.shape[0], n_p, -1),
                              p_sample.reshape(p_sample.shape[0], n_s, -1)], axis=1)

    def ple(h, hn, i):
        return _mm(hn, ple_w_gate[i], bias=ple_b_gate[i], act="sigmoid", mul=(ple_in[i], ple_w_proj[i]), res=h)

    (xn,) = _resnorm(h, norm_mix[0])
    proj = _mm(xn, lru_w_in[0])
    proj_s = proj[n_p:].reshape(bs, ts, 2 * w_lru)
    lw = (lru_conv_w[0], lru_conv_b[0], lru_w_a[0], lru_b_a[0], lru_w_i[0], lru_b_i[0], lru_lambda[0])
    hy_p, hl_p = _lru(proj, *lw, bsz=bp, t_len=sp, chunk=min(LRU_CHUNK, sp))
    zpad = jnp.zeros((bs, 2 * SUBLANES - ts - (CONV_WIDTH - 1), w_lru), F32)
    xb_s = jnp.concatenate([zpad, state_conv[0], proj_s[..., :w_lru]], axis=1)
    yb_s = jnp.concatenate([jnp.zeros((bs, 2 * SUBLANES - ts, w_lru), F32), proj_s[..., w_lru:]], axis=1)
    h0_rows = jnp.zeros((bs, 2 * SUBLANES, w_lru), F32).at[:, 2 * SUBLANES - ts - 1].set(state_h[0])
    seg_rows = bs * 2 * SUBLANES
    hy_s, hf_s = _lru(jnp.concatenate([xb_s, yb_s], axis=-1).reshape(seg_rows, 2 * w_lru), *lw,
                      bsz=1, t_len=seg_rows, h0_rows=h0_rows.reshape(seg_rows, w_lru))
    hy_s = hy_s.reshape(bs, 2 * SUBLANES, w_lru)[:, 2 * SUBLANES - ts:]
    hf_s = hf_s.reshape(bs, 2 * SUBLANES, w_lru)
    hy = jnp.concatenate([hy_p, hy_s.reshape(n_s, w_lru)], axis=0)
    h = _mm(hy, lru_w_out[0], res=h)
    new_h_prompt = hl_p[:, SUBLANES - 1][None]
    new_h_sample = hf_s[:, -1][None]
    new_conv_prompt = jnp.stack([proj[(b + 1) * sp - (CONV_WIDTH - 1):(b + 1) * sp, :w_lru] for b in range(bp)])[None]
    new_conv_sample = xb_s[:, 2 * SUBLANES - (CONV_WIDTH - 1):][None]

    (xn,) = _resnorm(h, norm_ffn[0])
    tm_ffn = _pick(h.shape[0], (FFN_TILE, 512, 256))
    nt_ffn = h.shape[0] // tm_ffn
    f = _ffn(xn, ffn_w_gate, ffn_w_up, ffn_w_down, jnp.zeros((nt_ffn,), jnp.int32),
             jnp.full((nt_ffn,), tm_ffn // MOE_SUB, jnp.int32), tm=tm_ffn, tf=MOE_TF, sub=MOE_SUB)
    h, hn = _resnorm(h, norm_ple[0], add=f, want_sum=True)
    h = ple(h, hn, 0)

    (xn,) = _resnorm(h, norm_mix[1])
    cq = _mm(xn, mla_wq_a[0])
    (cqn,) = _resnorm(cq, mla_q_norm[0])
    pos_all = jnp.concatenate([jnp.tile(jnp.arange(sp, dtype=F32), bp),
                               jnp.tile(jnp.arange(ts, dtype=F32) + past_len, bs)])
    cos, sin = _rope_tables(pos_all)
    zeros_r = jnp.zeros((n_p + n_s, LANES - QK_ROPE), F32)
    cos_k = jnp.concatenate([cos, cos, zeros_r], axis=1)
    sin_k = jnp.concatenate([sin, sin, zeros_r], axis=1)
    ones_n = jnp.ones((n_p + n_s, QK_NOPE), F32)
    cos_q = jnp.tile(jnp.concatenate([ones_n, cos_k], axis=1), (1, 2))
    sin_q = jnp.tile(jnp.concatenate([0.0 * ones_n, sin_k], axis=1), (1, 2))
    wq_b = mla_wq_b[0]
    cq_dim = wq_b.shape[0]
    wq_pe = wq_b[..., QK_NOPE:]
    zq = jnp.zeros((cq_dim, N_HEADS, LANES - QK_ROPE), F32)
    wq_main = jnp.concatenate([wq_b[..., :QK_NOPE], wq_pe, zq], axis=-1).reshape(cq_dim, N_HEADS * 2 * LANES)
    wq_rot = jnp.concatenate([jnp.zeros((cq_dim, N_HEADS, QK_NOPE), F32), _rot_cols(wq_pe), zq],
                             axis=-1).reshape(cq_dim, N_HEADS * 2 * LANES)
    q_cat = _mm(cqn, wq_main, w2=wq_rot, cos=cos_q, sin=sin_q, out_dtype=BF16, tn=4 * LANES)

    wkv_a = mla_wkv_a[0]
    ckv_raw = _mm(xn, wkv_a[:, :KV_LORA])
    ckv_f32, ckv_bf = _resnorm(ckv_raw, mla_kv_norm[0], want_f32=True)
    zk = jnp.zeros((d, LANES - QK_ROPE), F32)
    wk_pe = wkv_a[:, KV_LORA:]
    kpe_pad = _mm(xn, jnp.concatenate([wk_pe, zk], axis=1),
                  w2=jnp.concatenate([_rot_cols(wk_pe), zk], axis=1), cos=cos_k, sin=sin_k, tn=LANES)
    kpe_bf = kpe_pad.astype(BF16)
    new_ckv_prompt = ckv_f32[:n_p].reshape(1, bp, sp, KV_LORA)
    new_ckv_sample = ckv_f32[n_p:].reshape(1, bs, ts, KV_LORA)
    new_kpe_prompt = kpe_pad[:n_p, :QK_ROPE].reshape(1, bp, sp, QK_ROPE)
    new_kpe_sample = kpe_pad[n_p:, :QK_ROPE].reshape(1, bs, ts, QK_ROPE)

    wkv_b = mla_wkv_b[0]
    wkv_up = jnp.concatenate([wkv_b[..., :QK_NOPE].reshape(KV_LORA, -1), wkv_b[..., QK_NOPE:].reshape(KV_LORA, -1)],
                             axis=1)
    kv_up = _mm(ckv_bf, wkv_up, out_dtype=BF16, m=n_p)
    t_att = _pick(sp, (FLASH_TILE, 512, 256, 128))
    o_p = _flash(q_cat, kv_up, kpe_bf, bsz=bp, s_len=sp, t=t_att)

    q_s = q_cat[n_p:].reshape(bs, ts, N_HEADS, 2 * LANES)
    qn_s = jnp.transpose(q_s[..., :QK_NOPE], (2, 0, 1, 3)).reshape(N_HEADS, n_s, QK_NOPE)
    w_uk_t = jnp.transpose(wkv_b[..., :QK_NOPE], (1, 2, 0))
    q_lat = _mm(qn_s, w_uk_t, out_dtype=BF16)
    q_lat = jnp.transpose(q_lat.reshape(N_HEADS, bs, ts, KV_LORA), (1, 0, 2, 3)).reshape(bs, N_HEADS * ts, KV_LORA)
    q_pe = jnp.transpose(q_s[..., QK_NOPE:QK_NOPE + QK_ROPE], (0, 2, 1, 3)).reshape(bs, N_HEADS * ts, QK_ROPE)
    kpad = 2 * SUBLANES
    c_new = jnp.pad(ckv_bf[n_p:].reshape(bs, ts, KV_LORA), ((0, 0), (0, kpad - ts), (0, 0)))
    k_new = jnp.pad(kpe_bf[n_p:, :QK_ROPE].reshape(bs, ts, QK_ROPE), ((0, 0), (0, kpad - ts), (0, 0)))
    o_lat = _paged_attention(q_lat, q_pe, c_new, k_new, cache_ckv[0], jnp.swapaxes(cache_kpe[0], 1, 2), page_table,
                             nb=PAGED_BATCH, pp=min(PAGES_PER_STEP, page_table.shape[1]))
    o_lat = jnp.transpose(o_lat.reshape(bs, N_HEADS, ts, KV_LORA), (1, 0, 2, 3)).reshape(N_HEADS, n_s, KV_LORA)
    w_uv = jnp.transpose(wkv_b[..., QK_NOPE:], (1, 0, 2))
    o_s = _mm(o_lat, w_uv, out_dtype=BF16, tn=V_HEAD)
    o_s = jnp.transpose(o_s, (1, 0, 2)).reshape(n_s, N_HEADS * V_HEAD)
    o_all = jnp.concatenate([o_p, o_s], axis=0)
    h = _mm(o_all, mla_wo[0], res=h)

    (xn_f32,) = _resnorm(h, norm_ffn[1], want_f32=True, want_bf16=False)
    ys, pos, gates = _moe(xn_f32, moe_router[0], moe_w_gate[0], moe_w_up[0], moe_w_down[0])
    h, hn = _combine(h, ys, pos, gates, norm_ple[1], tm=_pick(h.shape[0], (COMBINE_TILE, 128, 64, 32, 16, 8)))
    h = ple(h, hn, 1)

    (y_prompt,) = _resnorm(h, norm_final, want_f32=True, want_bf16=False, rows=(0, n_p))
    (y_sample,) = _resnorm(h, norm_final, want_f32=True, want_bf16=False, rows=(n_p, n_s))
    y_prompt = y_prompt.reshape(bp, sp, d)
    y_sample = y_sample.reshape(bs, ts, d)
    return (y_prompt, y_sample, new_ckv_prompt, new_kpe_prompt, new_ckv_sample, new_kpe_sample,
            new_h_prompt, new_conv_prompt, new_h_sample, new_conv_sample)
```
